```python
import jax, jax.numpy as jnp
from jax import lax
import numpy as np

D_MODEL = 2048
BATCH = 8
SEQ = 2048
DEPTH = 2

N_MEM = 256
HEAD_DIM = 128
FOX_HEADS = 16
DIL_PATTERNS = ((128, 1), (512, 4), (2048, 16))
DIL_GROUP_HEADS = 6
DIL_HEADS = DIL_GROUP_HEADS * len(DIL_PATTERNS)
CROSS_HEADS = 4
D_FF = 5632
NUM_BUCKETS = 32
MAX_DISTANCE = 2048
Q_BLOCK = 128
N_MIXERS = 2
N_FOX_LAYERS = (DEPTH + 1) // 2
N_DIL_LAYERS = DEPTH // 2
RMS_EPS = 1e-6
NEG_INF = -1e30

kernel_name = "hybrid_fox_dilated_macaron"


def rmsnorm(x, g):
    xf = x.astype(jnp.float32)
    y = xf * lax.rsqrt(jnp.mean(xf * xf, axis=-1, keepdims=True) + RMS_EPS)
    return (y * g.astype(jnp.float32)).astype(x.dtype)


def swiglu(h, w_in, w_out):
    gu = h @ w_in
    gate, up = gu[..., :D_FF], gu[..., D_FF:]
    return (jax.nn.silu(gate) * up) @ w_out


def t5_bucket(dist):
    max_exact = NUM_BUCKETS // 2
    d = np.maximum(dist, 1).astype(np.float32)
    large = max_exact + (np.log(d / max_exact) / np.log(MAX_DISTANCE / max_exact)
                         * (NUM_BUCKETS - max_exact)).astype(np.int32)
    large = np.minimum(large, NUM_BUCKETS - 1)
    return np.where(dist < max_exact, dist, large).astype(np.int32)


def fox_mixer(h, w_in, b_f, w_out):
    B, S, _ = h.shape
    HD = FOX_HEADS * HEAD_DIM
    proj = h @ w_in
    q = proj[..., :HD].reshape(B, S, FOX_HEADS, HEAD_DIM)
    k = proj[..., HD:2 * HD].reshape(B, S, FOX_HEADS, HEAD_DIM)
    v = proj[..., 2 * HD:3 * HD].reshape(B, S, FOX_HEADS, HEAD_DIM)
    logf = jax.nn.log_sigmoid((proj[..., 3 * HD:] + b_f).astype(jnp.float32))
    c = jnp.cumsum(logf, axis=1).transpose(0, 2, 1)
    scale = HEAD_DIM ** -0.5
    pos = jnp.arange(S)

    def one_block(i):
        start = i * Q_BLOCK
        qb = lax.dynamic_slice_in_dim(q, start, Q_BLOCK, axis=1)
        cq = lax.dynamic_slice_in_dim(c, start, Q_BLOCK, axis=2)
        t = start + jnp.arange(Q_BLOCK)
        logits = (jnp.einsum('bqhd,bshd->bhqs', qb, k).astype(jnp.float32) * scale
                  + cq[..., None] - c[:, :, None, :])
        causal = pos[None, :] <= t[:, None]
        logits = jnp.where(causal[None, None], logits, NEG_INF)
        p = jax.nn.softmax(logits, axis=-1).astype(v.dtype)
        return jnp.einsum('bhqs,bshd->bqhd', p, v)

    o = lax.map(one_block, jnp.arange(S // Q_BLOCK))
    o = o.transpose(1, 0, 2, 3, 4).reshape(B, S, HD)
    return o @ w_out


def dilated_group_attention(q, k, v, bias_gk, dil, n_keys):
    B, S, G, dh = q.shape
    offs = jnp.asarray(np.arange(n_keys) * dil, jnp.int32)
    scale = dh ** -0.5

    def one_block(i):
        start = i * Q_BLOCK
        t = start + jnp.arange(Q_BLOCK)
        idx = t[:, None] - offs[None, :]
        valid = idx >= 0
        idx = jnp.maximum(idx, 0)
        qb = lax.dynamic_slice_in_dim(q, start, Q_BLOCK, axis=1)
        kb = k[:, idx]
        vb = v[:, idx]
        logits = (jnp.einsum('bqgd,bqkgd->bgqk', qb, kb).astype(jnp.float32) * scale
                  + bias_gk[None, :, None, :])
        logits = jnp.where(valid[None, None], logits, NEG_INF)
        m = jnp.max(logits, axis=-1, keepdims=True)
        e = jnp.exp(logits - m)
        den = jnp.sum(e, axis=-1, keepdims=True)
        p = (e / den).astype(v.dtype)
        out = jnp.einsum('bgqk,bqkgd->bqgd', p, vb)
        lse = (m + jnp.log(den))[..., 0].transpose(0, 2, 1)
        return out, lse

    o, lse = lax.map(one_block, jnp.arange(S // Q_BLOCK))
    o = o.transpose(1, 0, 2, 3, 4).reshape(B, S, G, dh)
    lse = lse.transpose(1, 0, 2, 3).reshape(B, S, G)
    return o, lse


def dilated_mixer(h, w_in, w_out, rel_bias):
    B, S, _ = h.shape
    n_g = len(DIL_PATTERNS)
    HD = DIL_HEADS * HEAD_DIM
    proj = h @ w_in
    q = proj[..., :HD].reshape(B, S, n_g, DIL_GROUP_HEADS, HEAD_DIM)
    k = proj[..., HD:2 * HD].reshape(B, S, n_g, DIL_GROUP_HEADS, HEAD_DIM)
    v = proj[..., 2 * HD:].reshape(B, S, n_g, DIL_GROUP_HEADS, HEAD_DIM)
    outs, lses = [], []
    for g, (window, dil) in enumerate(DIL_PATTERNS):
        n_keys = window // dil + 1
        buckets = t5_bucket(np.arange(n_keys) * dil)
        bias = rel_bias[buckets][:, g * DIL_GROUP_HEADS:(g + 1) * DIL_GROUP_HEADS]
        bias = bias.T.astype(jnp.float32)
        o, lse = dilated_group_attention(q[:, :, g], k[:, :, g], v[:, :, g], bias, dil, n_keys)
        outs.append(o)
        lses.append(lse)
    o = jnp.stack(outs, axis=2)
    alpha = jax.nn.softmax(jnp.stack(lses, axis=2), axis=2)
    mixed = (o * alpha[..., None].astype(o.dtype)).reshape(B, S, HD)
    return mixed @ w_out


def memory_cross_attention(h, mem_n, w_q, w_kv, w_out):
    B, S, _ = h.shape
    M = mem_n.shape[1]
    q = (h @ w_q).reshape(B, S, CROSS_HEADS, HEAD_DIM)
    kv = (mem_n @ w_kv).reshape(B, M, 2, CROSS_HEADS, HEAD_DIM)
    logits = jnp.einsum('bshd,bmhd->bhsm', q, kv[:, :, 0]).astype(jnp.float32) * HEAD_DIM ** -0.5
    p = jax.nn.softmax(logits, axis=-1).astype(kv.dtype)
    o = jnp.einsum('bhsm,bmhd->bshd', p, kv[:, :, 1]).reshape(B, S, CROSS_HEADS * HEAD_DIM)
    return o @ w_out


def setup_inputs(seed: int = 0) -> dict:
    key = jax.random.key(seed)
    ks = jax.random.split(key, 24)
    f32 = jnp.float32

    def nrm(k, shape, scale):
        return jax.random.normal(k, shape, f32) * scale

    def gain(k, shape):
        return 1.0 + 0.02 * jax.random.normal(k, shape, f32)

    D = D_MODEL
    fox_hd = FOX_HEADS * HEAD_DIM
    dil_hd = DIL_HEADS * HEAD_DIM
    cr_hd = CROSS_HEADS * HEAD_DIM
    return {
        "x": jax.random.normal(ks[0], (BATCH, SEQ, D), f32),
        "mem": jax.random.normal(ks[1], (BATCH, N_MEM, D), f32),
        "ffn1_norm": gain(ks[2], (DEPTH, D)),
        "ffn1_w_in": nrm(ks[3], (DEPTH, D, 2 * D_FF), D ** -0.5),
        "ffn1_w_out": nrm(ks[4], (DEPTH, D_FF, D), D_FF ** -0.5),
        "mix_norm": gain(ks[5], (DEPTH, D)),
        "fox_w_in": nrm(ks[6], (N_FOX_LAYERS, D, 3 * fox_hd + FOX_HEADS), D ** -0.5),
        "fox_b_f": 3.0 + 0.5 * jax.random.normal(ks[7], (N_FOX_LAYERS, FOX_HEADS), f32),
        "fox_w_out": nrm(ks[8], (N_FOX_LAYERS, fox_hd, D), fox_hd ** -0.5),
        "dil_w_in": nrm(ks[9], (N_DIL_LAYERS, D, 3 * dil_hd), D ** -0.5),
        "dil_w_out": nrm(ks[10], (N_DIL_LAYERS, dil_hd, D), dil_hd ** -0.5),
        "rel_bias": nrm(ks[11], (NUM_BUCKETS, DIL_HEADS), 0.3),
        "cross_norm": gain(ks[12], (DEPTH, D)),
        "mem_norm": gain(ks[13], (D,)),
        "cross_w_q": nrm(ks[14], (DEPTH, D, cr_hd), D ** -0.5),
        "cross_w_kv": nrm(ks[15], (DEPTH, D, 2 * cr_hd), D ** -0.5),
        "cross_w_out": nrm(ks[16], (DEPTH, cr_hd, D), cr_hd ** -0.5),
        "ffn2_norm": gain(ks[17], (DEPTH, D)),
        "ffn2_w_in": nrm(ks[18], (DEPTH, D, 2 * D_FF), D ** -0.5),
        "ffn2_w_out": nrm(ks[19], (DEPTH, D_FF, D), D_FF ** -0.5),
        "final_norm": gain(ks[20], (D,)),
    }


def reference(x, mem, ffn1_norm, ffn1_w_in, ffn1_w_out, mix_norm, fox_w_in, fox_b_f,
              fox_w_out, dil_w_in, dil_w_out, rel_bias, cross_norm, mem_norm, cross_w_q,
              cross_w_kv, cross_w_out, ffn2_norm, ffn2_w_in, ffn2_w_out, final_norm):
    mem_n = rmsnorm(mem, mem_norm)
    for i in range(DEPTH):
        x = x + 0.5 * swiglu(rmsnorm(x, ffn1_norm[i]), ffn1_w_in[i], ffn1_w_out[i])
        h = rmsnorm(x, mix_norm[i])
        j = i // N_MIXERS
        if i % N_MIXERS == 0:
            x = x + fox_mixer(h, fox_w_in[j], fox_b_f[j], fox_w_out[j])
        else:
            x = x + dilated_mixer(h, dil_w_in[j], dil_w_out[j], rel_bias)
        x = x + memory_cross_attention(rmsnorm(x, cross_norm[i]), mem_n,
                                       cross_w_q[i], cross_w_kv[i], cross_w_out[i])
        x = x + 0.5 * swiglu(rmsnorm(x, ffn2_norm[i]), ffn2_w_in[i], ffn2_w_out[i])
    return rmsnorm(x, final_norm)
```

```python
import functools

import numpy as np
import jax
import jax.numpy as jnp
from jax import lax
from jax.experimental import pallas as pl
from jax.experimental.pallas import tpu as pltpu

HEAD_DIM = 128
FOX_HEADS = 16
DIL_PATTERNS = ((128, 1), (512, 4), (2048, 16))
DIL_GROUP_HEADS = 6
CROSS_HEADS = 4
NUM_BUCKETS = 32
MAX_DISTANCE = 2048
RMS_EPS = 1e-6
NEG_INF = -1e30

LANES = 128
DIL_TILE = 128
VMEM_LIMIT_BYTES = 56 * 1024 * 1024

BF16 = jnp.bfloat16
F32 = jnp.float32
_NT = (((1,), (1,)), ((), ()))


def _params():
    return pltpu.CompilerParams(vmem_limit_bytes=VMEM_LIMIT_BYTES)


def _rmsnorm(x, g):
    return x * lax.rsqrt(jnp.mean(x * x, axis=-1, keepdims=True) + RMS_EPS) * g


def _log_sigmoid(z):
    return jnp.minimum(z, 0.0) - jnp.log1p(jnp.exp(-jnp.abs(z)))


def _rms_proj_kernel(x_ref, g_ref, w_ref, *rest, tn, q_cols, q_scale, with_gate):
    if with_gate:
        wf_ref, bf_ref, o_ref, logf_ref, h_ref = rest
    else:
        o_ref, h_ref = rest
    j = pl.program_id(1)

    @pl.when(j == 0)
    def _():
        h = _rmsnorm(x_ref[...], g_ref[...]).astype(BF16)
        h_ref[...] = h
        if with_gate:
            z = jnp.dot(h, wf_ref[...], preferred_element_type=F32) + bf_ref[...]
            logf_ref[...] = _log_sigmoid(z)

    acc = jnp.dot(h_ref[...], w_ref[...], preferred_element_type=F32)
    if q_cols:
        acc = acc * jnp.where(j * tn < q_cols, q_scale, 1.0).astype(F32)
    o_ref[...] = acc.astype(o_ref.dtype)


def _rms_proj(x, gain, w, *, tm, tn, out_dtype, q_cols=0, q_scale=1.0, gate=None):
    T, D = x.shape
    N = w.shape[1]
    assert T % tm == 0 and N % tn == 0 and q_cols % tn == 0
    in_specs = [
        pl.BlockSpec((tm, D), lambda i, j: (i, 0)),
        pl.BlockSpec((1, D), lambda i, j: (0, 0)),
        pl.BlockSpec((D, tn), lambda i, j: (0, j)),
    ]
    args = [x, gain.reshape(1, D), w]
    out_shape = [jax.ShapeDtypeStruct((T, N), out_dtype)]
    out_specs = [pl.BlockSpec((tm, tn), lambda i, j: (i, j))]
    if gate is not None:
        in_specs += [pl.BlockSpec((D, LANES), lambda i, j: (0, 0)),
                     pl.BlockSpec((1, LANES), lambda i, j: (0, 0))]
        args += list(gate)
        out_shape.append(jax.ShapeDtypeStruct((T, LANES), F32))
        out_specs.append(pl.BlockSpec((tm, LANES), lambda i, j: (i, 0)))
    res = pl.pallas_call(
        functools.partial(_rms_proj_kernel, tn=tn, q_cols=q_cols, q_scale=q_scale,
                          with_gate=gate is not None),
        grid=(T // tm, N // tn),
        in_specs=in_specs,
        out_specs=out_specs,
        out_shape=out_shape,
        scratch_shapes=[pltpu.VMEM((tm, D), BF16)],
        compiler_params=_params(),
        name="rms_proj_gate" if gate is not None else "rms_proj",
    )(*args)
    return res if gate is not None else res[0]


def _ffn_kernel(x_ref, g_ref, wg_ref, wu_ref, wo_ref, *rest, final_norm):
    if final_norm:
        fg_ref, o_ref, h_ref = rest
    else:
        o_ref, h_ref = rest
    j = pl.program_id(1)

    @pl.when(j == 0)
    def _():
        h_ref[...] = _rmsnorm(x_ref[...], g_ref[...]).astype(BF16)
        o_ref[...] = jnp.zeros_like(o_ref)

    h = h_ref[...]
    gate = jnp.dot(h, wg_ref[...], preferred_element_type=F32)
    up = jnp.dot(h, wu_ref[...], preferred_element_type=F32)
    act = (gate * jax.nn.sigmoid(gate) * up).astype(BF16)
    o_ref[...] += jnp.dot(act, wo_ref[...], preferred_element_type=F32)

    @pl.when(j == pl.num_programs(1) - 1)
    def _():
        y = x_ref[...] + 0.5 * o_ref[...]
        if final_norm:
            y = _rmsnorm(y, fg_ref[...])
        o_ref[...] = y


def _ffn(x, gain, w_in, w_out, *, tm, tf, final_gain=None):
    T, D = x.shape
    F = w_out.shape[0]
    assert T % tm == 0 and F % tf == 0
    nf = F // tf
    in_specs = [
        pl.BlockSpec((tm, D), lambda i, j: (i, 0)),
        pl.BlockSpec((1, D), lambda i, j: (0, 0)),
        pl.BlockSpec((D, tf), lambda i, j: (0, j)),
        pl.BlockSpec((D, tf), lambda i, j: (0, j + nf)),
        pl.BlockSpec((tf, D), lambda i, j: (j, 0)),
    ]
    args = [x, gain.reshape(1, D), w_in, w_in, w_out]
    if final_gain is not None:
        in_specs.append(pl.BlockSpec((1, D), lambda i, j: (0, 0)))
        args.append(final_gain.reshape(1, D))
    return pl.pallas_call(
        functools.partial(_ffn_kernel, final_norm=final_gain is not None),
        grid=(T // tm, nf),
        in_specs=in_specs,
        out_specs=pl.BlockSpec((tm, D), lambda i, j: (i, 0)),
        out_shape=jax.ShapeDtypeStruct((T, D), F32),
        scratch_shapes=[pltpu.VMEM((tm, D), BF16)],
        compiler_params=_params(),
        name="ffn_final" if final_gain is not None else "ffn",
    )(*args)


def _proj_res_kernel(*refs, n_a):
    a_refs, w_refs = refs[:n_a], refs[n_a:2 * n_a]
    x_ref, o_ref = refs[2 * n_a:]
    acc = x_ref[...]
    for a_ref, w_ref in zip(a_refs, w_refs):
        acc = acc + jnp.dot(a_ref[...], w_ref[...], preferred_element_type=F32)
    o_ref[...] = acc


def _proj_res(a_list, w, x, *, tm, tn):
    T, N = x.shape
    kg = a_list[0].shape[1]
    assert all(a.shape == (T, kg) for a in a_list) and kg * len(a_list) == w.shape[0]
    assert T % tm == 0 and N % tn == 0
    n_a = len(a_list)
    in_specs = [pl.BlockSpec((tm, kg), lambda i, j: (i, 0)) for _ in a_list]
    in_specs += [pl.BlockSpec((kg, tn), lambda i, j, g=g: (g, j)) for g in range(n_a)]
    in_specs.append(pl.BlockSpec((tm, tn), lambda i, j: (i, j)))
    return pl.pallas_call(
        functools.partial(_proj_res_kernel, n_a=n_a),
        grid=(T // tm, N // tn),
        in_specs=in_specs,
        out_specs=pl.BlockSpec((tm, tn), lambda i, j: (i, j)),
        out_shape=jax.ShapeDtypeStruct((T, N), F32),
        compiler_params=_params(),
        name="proj_res",
    )(*a_list, *([w] * n_a), x)


def _cumsum_kernel(tri_ref, x_ref, ccol_ref, crow_ref, *, heads):
    x = x_ref[0]
    hi = x.astype(BF16)
    r1 = x - hi.astype(F32)
    mid = r1.astype(BF16)
    lo = (r1 - mid.astype(F32)).astype(BF16)
    tri = tri_ref[...]
    c = (jnp.dot(tri, hi, preferred_element_type=F32)
         + jnp.dot(tri, mid, preferred_element_type=F32)
         + jnp.dot(tri, lo, preferred_element_type=F32))
    ccol_ref[0] = c
    crow_ref[0] = c.T[:heads]


def _fox_cumsum(logf, heads):
    B, S, _ = logf.shape
    tri = jnp.tril(jnp.ones((S, S), BF16))
    return pl.pallas_call(
        functools.partial(_cumsum_kernel, heads=heads),
        grid=(B,),
        in_specs=[pl.BlockSpec((S, S), lambda b: (0, 0)),
                  pl.BlockSpec((1, S, LANES), lambda b: (b, 0, 0))],
        out_specs=[pl.BlockSpec((1, S, LANES), lambda b: (b, 0, 0)),
                   pl.BlockSpec((1, heads, S), lambda b: (b, 0, 0))],
        out_shape=[jax.ShapeDtypeStruct((B, S, LANES), F32),
                   jax.ShapeDtypeStruct((B, heads, S), F32)],
        compiler_params=_params(),
        name="fox_cumsum",
    )(tri, logf)


def _fox_attn_kernel(q_ref, k_ref, v_ref, ccol_ref, crow_ref, o_ref, *, tile):
    h = pl.program_id(1)
    qi = pl.program_id(2)
    q = q_ref[0]
    lane = lax.broadcasted_iota(jnp.int32, (tile, LANES), 1)
    cq = jnp.sum(jnp.where(lane == h, ccol_ref[0], 0.0), axis=1, keepdims=True)

    def step(j, carry, diagonal):
        m, l, acc = carry
        start = pl.multiple_of(j * tile, tile)
        k = k_ref[0, pl.ds(start, tile), :]
        v = v_ref[0, pl.ds(start, tile), :]
        ck = crow_ref[0, 0, pl.ds(j, 1), :]
        s = lax.dot_general(q, k, _NT, preferred_element_type=F32)
        s = s + (cq - ck)
        if diagonal:
            row = lax.broadcasted_iota(jnp.int32, (tile, tile), 0)
            col = lax.broadcasted_iota(jnp.int32, (tile, tile), 1)
            s = jnp.where(col <= row, s, NEG_INF)
        m_new = jnp.maximum(m, jnp.max(s, axis=1, keepdims=True))
        alpha = jnp.exp(m - m_new)
        p = jnp.exp(s - m_new)
        l = alpha * l + jnp.sum(p, axis=1, keepdims=True)
        acc = alpha * acc + jnp.dot(p.astype(BF16), v, preferred_element_type=F32)
        return m_new, l, acc

    init = (jnp.full((tile, 1), NEG_INF, F32), jnp.zeros((tile, 1), F32),
            jnp.zeros((tile, HEAD_DIM), F32))
    carry = lax.fori_loop(0, qi, lambda j, c: step(j, c, False), init)
    _, l, acc = step(qi, carry, True)
    o_ref[0] = (acc / l).astype(o_ref.dtype)


def _fox_attention(qkv, ccol, crow, *, heads, tile):
    B, S, _ = qkv.shape
    assert S % tile == 0
    nq = S // tile
    crow4 = crow.reshape(B, heads, nq, tile)
    return pl.pallas_call(
        functools.partial(_fox_attn_kernel, tile=tile),
        grid=(B, heads, nq),
        in_specs=[
            pl.BlockSpec((1, tile, HEAD_DIM), lambda b, h, i: (b, i, h)),
            pl.BlockSpec((1, S, HEAD_DIM), lambda b, h, i: (b, 0, heads + h)),
            pl.BlockSpec((1, S, HEAD_DIM), lambda b, h, i: (b, 0, 2 * heads + h)),
            pl.BlockSpec((1, tile, LANES), lambda b, h, i: (b, i, 0)),
            pl.BlockSpec((1, 1, nq, tile), lambda b, h, i: (b, h, 0, 0)),
        ],
        out_specs=pl.BlockSpec((1, tile, HEAD_DIM), lambda b, h, i: (b, i, h)),
        out_shape=jax.ShapeDtypeStruct((B, S, heads * HEAD_DIM), BF16),
        compiler_params=_params(),
        name="fox_attn",
    )(qkv, qkv, qkv, ccol, crow4)


def _t5_bucket(dist):
    max_exact = NUM_BUCKETS // 2
    d = np.maximum(dist, 1).astype(np.float32)
    large = max_exact + (np.log(d / max_exact) / np.log(MAX_DISTANCE / max_exact)
                         * (NUM_BUCKETS - max_exact)).astype(np.int32)
    large = np.minimum(large, NUM_BUCKETS - 1)
    return np.where(dist < max_exact, dist, large).astype(np.int32)


def _dil_bias_tables(rel_bias, patterns, group_heads):
    a = np.arange(DIL_TILE)[:, None]
    b = np.arange(2 * DIL_TILE)[None, :]
    groups = []
    for g, (window, dil) in enumerate(patterns):
        assert window // dil == DIL_TILE
        strips = []
        for t in (0, 1):
            delta = t * DIL_TILE + a - b
            valid = (delta >= 0) & (delta <= DIL_TILE)
            bucket = _t5_bucket(np.clip(delta, 0, DIL_TILE) * dil)
            bias = rel_bias[bucket][:, :, g * group_heads:(g + 1) * group_heads]
            bias = jnp.where(valid[:, :, None], bias.astype(F32), NEG_INF)
            strips.append(jnp.transpose(bias, (2, 0, 1)))
        groups.append(jnp.stack(strips, axis=1))
    return jnp.stack(groups)


def _dil_attn_kernel(*refs, dils, seq):
    n_g = len(dils)
    q_refs, k_refs, v_refs = refs[:n_g], refs[n_g:2 * n_g], refs[2 * n_g:3 * n_g]
    t_ref = refs[3 * n_g]
    o_refs = refs[3 * n_g + 1:4 * n_g + 1]
    o_sc, lse_sc = refs[4 * n_g + 1:]

    def rows(start, size, stride):
        return pl.ds(start, size) if stride == 1 else pl.ds(start, size, stride=stride)

    for g, d in enumerate(dils):
        cls_len = seq // d
        n_keys = min(2 * DIL_TILE, cls_len)
        for r in range(d):
            for u0 in range(0, cls_len, DIL_TILE):
                s0 = min(max(u0 - DIL_TILE, 0), cls_len - n_keys)
                strip = (u0 - s0) // DIL_TILE
                q_rows = rows(r + d * u0, DIL_TILE, d)
                k_rows = rows(r + d * s0, n_keys, d)
                q = q_refs[g][0, q_rows, :].astype(BF16)
                k = k_refs[g][0, k_rows, :].astype(BF16)
                v = v_refs[g][0, k_rows, :].astype(BF16)
                s = lax.dot_general(q, k, _NT, preferred_element_type=F32)
                s = s + t_ref[g, 0, strip][:, :n_keys]
                m = jnp.max(s, axis=1, keepdims=True)
                p = jnp.exp(s - m)
                l = jnp.sum(p, axis=1, keepdims=True)
                o = jnp.dot(p.astype(BF16), v, preferred_element_type=F32) / l
                o_sc[g, q_rows, :] = o
                lse_sc[g, q_rows, :] = jnp.broadcast_to(m + jnp.log(l), (DIL_TILE, LANES))

    lse = [lse_sc[g] for g in range(n_g)]
    mx = functools.reduce(jnp.maximum, lse)
    e = [jnp.exp(x - mx) for x in lse]
    inv = 1.0 / functools.reduce(lambda x, y: x + y, e)
    for g in range(n_g):
        o_refs[g][0] = (o_sc[g] * (e[g] * inv)).astype(o_refs[g].dtype)


def _dil_attention(qkv, tables, *, patterns, group_heads):
    B, S, _ = qkv.shape
    n_g = len(patterns)
    dils = tuple(d for _, d in patterns)
    assert all(S % (d * DIL_TILE) == 0 for d in dils)
    nh = n_g * group_heads

    def head_spec(part, g):
        return pl.BlockSpec((1, S, HEAD_DIM),
                            lambda b, h, part=part, g=g: (b, 0, part * nh + g * group_heads + h))

    in_specs = [head_spec(part, g) for part in range(3) for g in range(n_g)]
    in_specs.append(pl.BlockSpec((n_g, 1, 2, DIL_TILE, 2 * DIL_TILE),
                                 lambda b, h: (0, h, 0, 0, 0)))
    out_spec = pl.BlockSpec((1, S, HEAD_DIM), lambda b, h: (b, 0, h))
    return pl.pallas_call(
        functools.partial(_dil_attn_kernel, dils=dils, seq=S),
        grid=(B, group_heads),
        in_specs=in_specs,
        out_specs=[out_spec] * n_g,
        out_shape=[jax.ShapeDtypeStruct((B, S, group_heads * HEAD_DIM), BF16)] * n_g,
        scratch_shapes=[pltpu.VMEM((n_g, S, HEAD_DIM), F32), pltpu.VMEM((n_g, S, LANES), F32)],
        compiler_params=_params(),
        name="dil_attn",
    )(*([qkv] * (3 * n_g)), tables)


def _cross_kernel(x_ref, g_ref, wq_ref, kv_ref, wo_ref, o_ref, *, heads, scale):
    x = x_ref[0]
    h = _rmsnorm(x, g_ref[...]).astype(BF16)
    q = (jnp.dot(h, wq_ref[...], preferred_element_type=F32) * scale).astype(BF16)
    kv = kv_ref[0]
    outs = []
    for hd in range(heads):
        qh = q[:, hd * HEAD_DIM:(hd + 1) * HEAD_DIM]
        kh = kv[:, hd * HEAD_DIM:(hd + 1) * HEAD_DIM]
        vh = kv[:, (heads + hd) * HEAD_DIM:(heads + hd + 1) * HEAD_DIM]
        s = lax.dot_general(qh, kh, _NT, preferred_element_type=F32)
        p = jnp.exp(s - jnp.max(s, axis=1, keepdims=True))
        l = jnp.sum(p, axis=1, keepdims=True)
        outs.append((jnp.dot(p.astype(BF16), vh, preferred_element_type=F32) / l).astype(BF16))
    o = jnp.concatenate(outs, axis=1)
    o_ref[0] = x + jnp.dot(o, wo_ref[...], preferred_element_type=F32)


def _cross_attention(x, gain, wq, kv, wo, *, heads, tm):
    B, S, D = x.shape
    M = kv.shape[1]
    hd = heads * HEAD_DIM
    assert S % tm == 0
    return pl.pallas_call(
        functools.partial(_cross_kernel, heads=heads, scale=HEAD_DIM ** -0.5),
        grid=(B, S // tm),
        in_specs=[
            pl.BlockSpec((1, tm, D), lambda b, i: (b, i, 0)),
            pl.BlockSpec((1, D), lambda b, i: (0, 0)),
            pl.BlockSpec((D, hd), lambda b, i: (0, 0)),
            pl.BlockSpec((1, M, 2 * hd), lambda b, i: (b, 0, 0)),
            pl.BlockSpec((hd, D), lambda b, i: (0, 0)),
        ],
        out_specs=pl.BlockSpec((1, tm, D), lambda b, i: (b, i, 0)),
        out_shape=jax.ShapeDtypeStruct((B, S, D), F32),
        compiler_params=_params(),
        name="cross_attn",
    )(x, gain.reshape(1, D), wq, kv, wo)


def _fox_mixer(x, gain, w_in, b_f, w_out, *, heads, attn_tile, tm):
    B, S, D = x.shape
    hd = heads * HEAD_DIM
    w_qkv = w_in[:, :3 * hd].astype(BF16)
    w_f = jnp.pad(w_in[:, 3 * hd:], ((0, 0), (0, LANES - heads))).astype(BF16)
    b_pad = jnp.pad(b_f, (0, LANES - heads)).reshape(1, LANES).astype(F32)
    qkv, logf = _rms_proj(x.reshape(B * S, D), gain, w_qkv, tm=tm, tn=min(512, hd), out_dtype=BF16,
                          q_cols=hd, q_scale=HEAD_DIM ** -0.5, gate=(w_f, b_pad))
    ccol, crow = _fox_cumsum(logf.reshape(B, S, LANES), heads)
    o = _fox_attention(qkv.reshape(B, S, 3 * hd), ccol, crow, heads=heads, tile=attn_tile)
    y = _proj_res([o.reshape(B * S, hd)], w_out.astype(BF16), x.reshape(B * S, D), tm=tm,
                  tn=min(1024, D))
    return y.reshape(B, S, D)


def _dilated_mixer(x, gain, w_in, w_out, rel_bias, *, patterns, group_heads, tm):
    B, S, D = x.shape
    n_g = len(patterns)
    hd = n_g * group_heads * HEAD_DIM
    qkv = _rms_proj(x.reshape(B * S, D), gain, w_in.astype(BF16), tm=tm,
                    tn=group_heads * HEAD_DIM, out_dtype=F32, q_cols=hd, q_scale=HEAD_DIM ** -0.5)
    tables = _dil_bias_tables(rel_bias, patterns, group_heads)
    outs = _dil_attention(qkv.reshape(B, S, 3 * hd), tables, patterns=patterns,
                          group_heads=group_heads)
    outs = [o.reshape(B * S, group_heads * HEAD_DIM) for o in outs]
    y = _proj_res(outs, w_out.astype(BF16), x.reshape(B * S, D), tm=tm, tn=min(1024, D))
    return y.reshape(B, S, D)


def _forward(x, mem, ffn1_norm, ffn1_w_in, ffn1_w_out, mix_norm, fox_w_in, fox_b_f, fox_w_out,
             dil_w_in, dil_w_out, rel_bias, cross_norm, mem_norm, cross_w_q, cross_w_kv,
             cross_w_out, ffn2_norm, ffn2_w_in, ffn2_w_out, final_norm, *,
             patterns, fox_heads, group_heads, cross_heads, tm, tf, attn_tile, cross_tm):
    B, S, D = x.shape
    M = mem.shape[1]
    depth = ffn1_norm.shape[0]
    n_mixers = 2
    for i in range(depth):
        x = _ffn(x.reshape(B * S, D), ffn1_norm[i], ffn1_w_in[i].astype(BF16),
                 ffn1_w_out[i].astype(BF16), tm=tm, tf=tf).reshape(B, S, D)
        j = i // n_mixers
        if i % n_mixers == 0:
            x = _fox_mixer(x, mix_norm[i], fox_w_in[j], fox_b_f[j], fox_w_out[j],
                           heads=fox_heads, attn_tile=attn_tile, tm=tm)
        else:
            x = _dilated_mixer(x, mix_norm[i], dil_w_in[j], dil_w_out[j], rel_bias,
                               patterns=patterns, group_heads=group_heads, tm=tm)
        kv = _rms_proj(mem.reshape(B * M, D), mem_norm, cross_w_kv[i].astype(BF16),
                       tm=min(tm, B * M), tn=min(512, cross_heads * HEAD_DIM), out_dtype=BF16)
        x = _cross_attention(x, cross_norm[i], cross_w_q[i].astype(BF16),
                             kv.reshape(B, M, -1), cross_w_out[i].astype(BF16),
                             heads=cross_heads, tm=cross_tm)
        x = _ffn(x.reshape(B * S, D), ffn2_norm[i], ffn2_w_in[i].astype(BF16),
                 ffn2_w_out[i].astype(BF16), tm=tm, tf=tf,
                 final_gain=final_norm if i == depth - 1 else None).reshape(B, S, D)
    return x


def kernel(x, mem, ffn1_norm, ffn1_w_in, ffn1_w_out, mix_norm, fox_w_in, fox_b_f, fox_w_out, dil_w_in, dil_w_out, rel_bias, cross_norm, mem_norm, cross_w_q, cross_w_kv, cross_w_out, ffn2_norm, ffn2_w_in, ffn2_w_out, final_norm):
    return _forward(x, mem, ffn1_norm, ffn1_w_in, ffn1_w_out, mix_norm, fox_w_in, fox_b_f,
                    fox_w_out, dil_w_in, dil_w_out, rel_bias, cross_norm, mem_norm, cross_w_q,
                    cross_w_kv, cross_w_out, ffn2_norm, ffn2_w_in, ffn2_w_out, final_norm,
                    patterns=DIL_PATTERNS, fox_heads=FOX_HEADS, group_heads=DIL_GROUP_HEADS,
                    cross_heads=CROSS_HEADS, tm=512, tf=512, attn_tile=512, cross_tm=512)
```

```python
import functools

import numpy as np
import jax
import jax.numpy as jnp
from jax import lax
from jax.experimental import pallas as pl
from jax.experimental.pallas import tpu as pltpu

HEAD_DIM = 128
FOX_HEADS = 16
DIL_PATTERNS = ((128, 1), (512, 4), (2048, 16))
DIL_GROUP_HEADS = 6
CROSS_HEADS = 4
NUM_BUCKETS = 32
MAX_DISTANCE = 2048
RMS_EPS = 1e-6
NEG_INF = -1e30

LANES = 128
DIL_TILE = 128
VMEM_LIMIT_BYTES = 56 * 1024 * 1024

BF16 = jnp.bfloat16
F32 = jnp.float32
_NT = (((1,), (1,)), ((), ()))
LOG2E = 1.4426950408889634


def _params():
    return pltpu.CompilerParams(vmem_limit_bytes=VMEM_LIMIT_BYTES)


def _rmsnorm(x, g):
    return x * lax.rsqrt(jnp.mean(x * x, axis=-1, keepdims=True) + RMS_EPS) * g


def _log_sigmoid(z):
    return jnp.minimum(z, 0.0) - jnp.log1p(jnp.exp(-jnp.abs(z)))


def _rms_proj_kernel(x_ref, g_ref, w_ref, *rest, tn, q_cols, q_scale, with_gate):
    if with_gate:
        wf_ref, bf_ref, o_ref, logf_ref, h_ref = rest
    else:
        o_ref, h_ref = rest
    j = pl.program_id(1)

    @pl.when(j == 0)
    def _():
        h = _rmsnorm(x_ref[...], g_ref[...]).astype(BF16)
        h_ref[...] = h
        if with_gate:
            z = jnp.dot(h, wf_ref[...], preferred_element_type=F32) + bf_ref[...]
            logf_ref[...] = _log_sigmoid(z)

    acc = jnp.dot(h_ref[...], w_ref[...], preferred_element_type=F32)
    if q_cols:
        acc = acc * jnp.where(j * tn < q_cols, q_scale, 1.0).astype(F32)
    o_ref[...] = acc.astype(o_ref.dtype)


def _rms_proj(x, gain, w, *, tm, tn, out_dtype, q_cols=0, q_scale=1.0, gate=None):
    T, D = x.shape
    N = w.shape[1]
    assert T % tm == 0 and N % tn == 0 and q_cols % tn == 0
    in_specs = [
        pl.BlockSpec((tm, D), lambda i, j: (i, 0)),
        pl.BlockSpec((1, D), lambda i, j: (0, 0)),
        pl.BlockSpec((D, tn), lambda i, j: (0, j)),
    ]
    args = [x, gain.reshape(1, D), w]
    out_shape = [jax.ShapeDtypeStruct((T, N), out_dtype)]
    out_specs = [pl.BlockSpec((tm, tn), lambda i, j: (i, j))]
    if gate is not None:
        in_specs += [pl.BlockSpec((D, LANES), lambda i, j: (0, 0)),
                     pl.BlockSpec((1, LANES), lambda i, j: (0, 0))]
        args += list(gate)
        out_shape.append(jax.ShapeDtypeStruct((T, LANES), F32))
        out_specs.append(pl.BlockSpec((tm, LANES), lambda i, j: (i, 0)))
    res = pl.pallas_call(
        functools.partial(_rms_proj_kernel, tn=tn, q_cols=q_cols, q_scale=q_scale,
                          with_gate=gate is not None),
        grid=(T // tm, N // tn),
        in_specs=in_specs,
        out_specs=out_specs,
        out_shape=out_shape,
        scratch_shapes=[pltpu.VMEM((tm, D), BF16)],
        compiler_params=_params(),
        name="rms_proj_gate" if gate is not None else "rms_proj",
    )(*args)
    return res if gate is not None else res[0]


def _ffn_kernel(x_ref, g_ref, wg_ref, wu_ref, wo_ref, *rest, final_norm):
    if final_norm:
        fg_ref, o_ref, h_ref = rest
    else:
        o_ref, h_ref = rest
    j = pl.program_id(1)

    @pl.when(j == 0)
    def _():
        h_ref[...] = _rmsnorm(x_ref[...], g_ref[...]).astype(BF16)
        o_ref[...] = jnp.zeros_like(o_ref)

    h = h_ref[...]
    gate = jnp.dot(h, wg_ref[...], preferred_element_type=F32)
    up = jnp.dot(h, wu_ref[...], preferred_element_type=F32)
    act = (gate * jax.nn.sigmoid(gate) * up).astype(BF16)
    o_ref[...] += jnp.dot(act, wo_ref[...], preferred_element_type=F32)

    @pl.when(j == pl.num_programs(1) - 1)
    def _():
        y = x_ref[...] + 0.5 * o_ref[...]
        if final_norm:
            y = _rmsnorm(y, fg_ref[...])
        o_ref[...] = y


def _ffn(x, gain, w_in, w_out, *, tm, tf, final_gain=None):
    T, D = x.shape
    F = w_out.shape[0]
    assert T % tm == 0 and F % tf == 0
    nf = F // tf
    in_specs = [
        pl.BlockSpec((tm, D), lambda i, j: (i, 0)),
        pl.BlockSpec((1, D), lambda i, j: (0, 0)),
        pl.BlockSpec((D, tf), lambda i, j: (0, j)),
        pl.BlockSpec((D, tf), lambda i, j: (0, j + nf)),
        pl.BlockSpec((tf, D), lambda i, j: (j, 0)),
    ]
    args = [x, gain.reshape(1, D), w_in, w_in, w_out]
    if final_gain is not None:
        in_specs.append(pl.BlockSpec((1, D), lambda i, j: (0, 0)))
        args.append(final_gain.reshape(1, D))
    return pl.pallas_call(
        functools.partial(_ffn_kernel, final_norm=final_gain is not None),
        grid=(T // tm, nf),
        in_specs=in_specs,
        out_specs=pl.BlockSpec((tm, D), lambda i, j: (i, 0)),
        out_shape=jax.ShapeDtypeStruct((T, D), F32),
        scratch_shapes=[pltpu.VMEM((tm, D), BF16)],
        compiler_params=_params(),
        name="ffn_final" if final_gain is not None else "ffn",
    )(*args)


def _proj_res_kernel(*refs, n_a):
    a_refs, w_refs = refs[:n_a], refs[n_a:2 * n_a]
    x_ref, o_ref = refs[2 * n_a:]
    acc = x_ref[...]
    for a_ref, w_ref in zip(a_refs, w_refs):
        acc = acc + jnp.dot(a_ref[...], w_ref[...], preferred_element_type=F32)
    o_ref[...] = acc


def _proj_res(a_list, w, x, *, tm, tn):
    T, N = x.shape
    kg = a_list[0].shape[1]
    assert all(a.shape == (T, kg) for a in a_list) and kg * len(a_list) == w.shape[0]
    assert T % tm == 0 and N % tn == 0
    n_a = len(a_list)
    in_specs = [pl.BlockSpec((tm, kg), lambda i, j: (i, 0)) for _ in a_list]
    in_specs += [pl.BlockSpec((kg, tn), lambda i, j, g=g: (g, j)) for g in range(n_a)]
    in_specs.append(pl.BlockSpec((tm, tn), lambda i, j: (i, j)))
    return pl.pallas_call(
        functools.partial(_proj_res_kernel, n_a=n_a),
        grid=(T // tm, N // tn),
        in_specs=in_specs,
        out_specs=pl.BlockSpec((tm, tn), lambda i, j: (i, j)),
        out_shape=jax.ShapeDtypeStruct((T, N), F32),
        compiler_params=_params(),
        name="proj_res",
    )(*a_list, *([w] * n_a), x)


def _cumsum_kernel(tri_ref, x_ref, ccol_ref, crow_ref, *, heads):
    x = x_ref[0] * LOG2E
    hi = x.astype(BF16)
    r1 = x - hi.astype(F32)
    mid = r1.astype(BF16)
    lo = (r1 - mid.astype(F32)).astype(BF16)
    tri = tri_ref[...]
    c = (jnp.dot(tri, hi, preferred_element_type=F32)
         + jnp.dot(tri, mid, preferred_element_type=F32)
         + jnp.dot(tri, lo, preferred_element_type=F32))
    ccol_ref[0] = c
    crow_ref[0] = c.T[:heads]


def _fox_cumsum(logf, heads):
    B, S, _ = logf.shape
    tri = jnp.tril(jnp.ones((S, S), BF16))
    return pl.pallas_call(
        functools.partial(_cumsum_kernel, heads=heads),
        grid=(B,),
        in_specs=[pl.BlockSpec((S, S), lambda b: (0, 0)),
                  pl.BlockSpec((1, S, LANES), lambda b: (b, 0, 0))],
        out_specs=[pl.BlockSpec((1, S, LANES), lambda b: (b, 0, 0)),
                   pl.BlockSpec((1, heads, S), lambda b: (b, 0, 0))],
        out_shape=[jax.ShapeDtypeStruct((B, S, LANES), F32),
                   jax.ShapeDtypeStruct((B, heads, S), F32)],
        compiler_params=_params(),
        name="fox_cumsum",
    )(tri, logf)


def _fox_attn_kernel(q_ref, k_ref, v_ref, ccol_ref, crow_ref, o_ref, *, tile, hpb):
    hb = pl.program_id(1)
    qi = pl.program_id(2)
    lane = lax.broadcasted_iota(jnp.int32, (tile, LANES), 1)
    ccol = ccol_ref[0]
    heads = range(hpb)
    q = [q_ref[0, :, e * HEAD_DIM:(e + 1) * HEAD_DIM] for e in heads]
    cq = [jnp.sum(jnp.where(lane == hb * hpb + e, ccol, 0.0), axis=1, keepdims=True)
          for e in heads]

    def step(j, carry, diagonal):
        start = pl.multiple_of(j * tile, tile)
        out = []
        for e in heads:
            m, l, acc = carry[e]
            k = k_ref[0, pl.ds(start, tile), e * HEAD_DIM:(e + 1) * HEAD_DIM]
            v = v_ref[0, pl.ds(start, tile), e * HEAD_DIM:(e + 1) * HEAD_DIM]
            ck = crow_ref[0, e, pl.ds(j, 1), :]
            s = lax.dot_general(q[e], k, _NT, preferred_element_type=F32) - ck
            if diagonal:
                row = lax.broadcasted_iota(jnp.int32, (tile, tile), 0)
                col = lax.broadcasted_iota(jnp.int32, (tile, tile), 1)
                s = jnp.where(col <= row, s, NEG_INF)
            m_new = jnp.maximum(m, jnp.max(s, axis=1, keepdims=True) + cq[e])
            alpha = jnp.exp2(m - m_new)
            p = jnp.exp2(s - (m_new - cq[e]))
            l = alpha * l + jnp.sum(p, axis=1, keepdims=True)
            acc = alpha * acc + jnp.dot(p.astype(BF16), v, preferred_element_type=F32)
            out.append((m_new, l, acc))
        return tuple(out)

    init = tuple((jnp.full((tile, 1), NEG_INF, F32), jnp.zeros((tile, 1), F32),
                  jnp.zeros((tile, HEAD_DIM), F32)) for _ in heads)
    carry = lax.fori_loop(0, qi, lambda j, c: step(j, c, False), init)
    carry = step(qi, carry, True)
    for e in heads:
        _, l, acc = carry[e]
        o_ref[0, :, e * HEAD_DIM:(e + 1) * HEAD_DIM] = (acc / l).astype(o_ref.dtype)


def _fox_attention(qkv, ccol, crow, *, heads, tile, hpb=2):
    B, S, _ = qkv.shape
    assert S % tile == 0 and heads % hpb == 0
    nq = S // tile
    nhb = heads // hpb
    w = hpb * HEAD_DIM
    crow4 = crow.reshape(B, heads, nq, tile)
    return pl.pallas_call(
        functools.partial(_fox_attn_kernel, tile=tile, hpb=hpb),
        grid=(B, nhb, nq),
        in_specs=[
            pl.BlockSpec((1, tile, w), lambda b, h, i: (b, i, h)),
            pl.BlockSpec((1, S, w), lambda b, h, i: (b, 0, nhb + h)),
            pl.BlockSpec((1, S, w), lambda b, h, i: (b, 0, 2 * nhb + h)),
            pl.BlockSpec((1, tile, LANES), lambda b, h, i: (b, i, 0)),
            pl.BlockSpec((1, hpb, nq, tile), lambda b, h, i: (b, h, 0, 0)),
        ],
        out_specs=pl.BlockSpec((1, tile, w), lambda b, h, i: (b, i, h)),
        out_shape=jax.ShapeDtypeStruct((B, S, heads * HEAD_DIM), BF16),
        compiler_params=_params(),
        name="fox_attn",
    )(qkv, qkv, qkv, ccol, crow4)


def _t5_bucket(dist):
    max_exact = NUM_BUCKETS // 2
    d = np.maximum(dist, 1).astype(np.float32)
    large = max_exact + (np.log(d / max_exact) / np.log(MAX_DISTANCE / max_exact)
                         * (NUM_BUCKETS - max_exact)).astype(np.int32)
    large = np.minimum(large, NUM_BUCKETS - 1)
    return np.where(dist < max_exact, dist, large).astype(np.int32)


def _dil_bias_tables(rel_bias, patterns, group_heads):
    period = 4 * DIL_TILE
    c = np.arange(period)
    back = c < 2 * DIL_TILE
    groups = []
    for g, (window, dil) in enumerate(patterns):
        assert window // dil == DIL_TILE
        strips = []
        for t in (0, 1):
            delta = np.where(back, t * DIL_TILE - c, t * DIL_TILE + period - c)
            valid = (delta >= 0) & (delta <= DIL_TILE) & (back | (c > period - DIL_TILE))
            bucket = _t5_bucket(np.clip(delta, 0, DIL_TILE) * dil)
            vec = rel_bias[bucket][:, g * group_heads:(g + 1) * group_heads].astype(F32)
            vec = jnp.where(valid[:, None], vec, NEG_INF).T
            flat = jnp.tile(vec, (1, DIL_TILE))[:, :DIL_TILE * (period - 1)]
            strips.append(flat.reshape(group_heads, DIL_TILE, period - 1)[:, :, :2 * DIL_TILE])
        groups.append(jnp.stack(strips, axis=1))
    return jnp.stack(groups)


def _dil_attn_kernel(*refs, dils, seq):
    n_g = len(dils)
    q_refs, k_refs, v_refs = refs[:n_g], refs[n_g:2 * n_g], refs[2 * n_g:3 * n_g]
    t_ref = refs[3 * n_g]
    o_refs = refs[3 * n_g + 1:4 * n_g + 1]
    o_sc, lse_sc = refs[4 * n_g + 1:]

    def rows(start, size, stride):
        return pl.ds(start, size) if stride == 1 else pl.ds(start, size, stride=stride)

    for g, d in enumerate(dils):
        cls_len = seq // d
        n_keys = min(2 * DIL_TILE, cls_len)
        for r in range(d):
            for u0 in range(0, cls_len, DIL_TILE):
                s0 = min(max(u0 - DIL_TILE, 0), cls_len - n_keys)
                strip = (u0 - s0) // DIL_TILE
                q_rows = rows(r + d * u0, DIL_TILE, d)
                k_rows = rows(r + d * s0, n_keys, d)
                q = q_refs[g][0, q_rows, :].astype(BF16)
                k = k_refs[g][0, k_rows, :].astype(BF16)
                v = v_refs[g][0, k_rows, :].astype(BF16)
                s = lax.dot_general(q, k, _NT, preferred_element_type=F32)
                s = s + t_ref[g, 0, strip][:, :n_keys]
                m = jnp.max(s, axis=1, keepdims=True)
                p = jnp.exp(s - m)
                l = jnp.sum(p, axis=1, keepdims=True)
                o = jnp.dot(p.astype(BF16), v, preferred_element_type=F32) / l
                o_sc[g, q_rows, :] = o
                lse_sc[g, q_rows, :] = jnp.broadcast_to(m + jnp.log(l), (DIL_TILE, LANES))

    lse = [lse_sc[g] for g in range(n_g)]
    mx = functools.reduce(jnp.maximum, lse)
    e = [jnp.exp(x - mx) for x in lse]
    inv = 1.0 / functools.reduce(lambda x, y: x + y, e)
    for g in range(n_g):
        o_refs[g][0] = (o_sc[g] * (e[g] * inv)).astype(o_refs[g].dtype)


def _dil_attention(qkv, tables, *, patterns, group_heads):
    B, S, _ = qkv.shape
    n_g = len(patterns)
    dils = tuple(d for _, d in patterns)
    assert all(S % (d * DIL_TILE) == 0 for d in dils)
    nh = n_g * group_heads

    def head_spec(part, g):
        return pl.BlockSpec((1, S, HEAD_DIM),
                            lambda b, h, part=part, g=g: (b, 0, part * nh + g * group_heads + h))

    in_specs = [head_spec(part, g) for part in range(3) for g in range(n_g)]
    in_specs.append(pl.BlockSpec((n_g, 1, 2, DIL_TILE, 2 * DIL_TILE),
                                 lambda b, h: (0, h, 0, 0, 0)))
    out_spec = pl.BlockSpec((1, S, HEAD_DIM), lambda b, h: (b, 0, h))
    return pl.pallas_call(
        functools.partial(_dil_attn_kernel, dils=dils, seq=S),
        grid=(B, group_heads),
        in_specs=in_specs,
        out_specs=[out_spec] * n_g,
        out_shape=[jax.ShapeDtypeStruct((B, S, group_heads * HEAD_DIM), BF16)] * n_g,
        scratch_shapes=[pltpu.VMEM((n_g, S, HEAD_DIM), F32), pltpu.VMEM((n_g, S, LANES), F32)],
        compiler_params=_params(),
        name="dil_attn",
    )(*([qkv] * (3 * n_g)), tables)


def _cross_kernel(x_ref, g_ref, wq_ref, kv_ref, wo_ref, o_ref, *, heads, scale):
    x = x_ref[0]
    h = _rmsnorm(x, g_ref[...]).astype(BF16)
    q = (jnp.dot(h, wq_ref[...], preferred_element_type=F32) * scale).astype(BF16)
    kv = kv_ref[0]
    outs = []
    for hd in range(heads):
        qh = q[:, hd * HEAD_DIM:(hd + 1) * HEAD_DIM]
        kh = kv[:, hd * HEAD_DIM:(hd + 1) * HEAD_DIM]
        vh = kv[:, (heads + hd) * HEAD_DIM:(heads + hd + 1) * HEAD_DIM]
        s = lax.dot_general(qh, kh, _NT, preferred_element_type=F32)
        p = jnp.exp(s - jnp.max(s, axis=1, keepdims=True))
        l = jnp.sum(p, axis=1, keepdims=True)
        outs.append((jnp.dot(p.astype(BF16), vh, preferred_element_type=F32) / l).astype(BF16))
    o = jnp.concatenate(outs, axis=1)
    o_ref[0] = x + jnp.dot(o, wo_ref[...], preferred_element_type=F32)


def _cross_attention(x, gain, wq, kv, wo, *, heads, tm):
    B, S, D = x.shape
    M = kv.shape[1]
    hd = heads * HEAD_DIM
    assert S % tm == 0
    return pl.pallas_call(
        functools.partial(_cross_kernel, heads=heads, scale=HEAD_DIM ** -0.5),
        grid=(B, S // tm),
        in_specs=[
            pl.BlockSpec((1, tm, D), lambda b, i: (b, i, 0)),
            pl.BlockSpec((1, D), lambda b, i: (0, 0)),
            pl.BlockSpec((D, hd), lambda b, i: (0, 0)),
            pl.BlockSpec((1, M, 2 * hd), lambda b, i: (b, 0, 0)),
            pl.BlockSpec((hd, D), lambda b, i: (0, 0)),
        ],
        out_specs=pl.BlockSpec((1, tm, D), lambda b, i: (b, i, 0)),
        out_shape=jax.ShapeDtypeStruct((B, S, D), F32),
        compiler_params=_params(),
        name="cross_attn",
    )(x, gain.reshape(1, D), wq, kv, wo)


def _fox_mixer(x, gain, w_in, b_f, w_out, *, heads, attn_tile, tm):
    B, S, D = x.shape
    hd = heads * HEAD_DIM
    w_qkv = w_in[:, :3 * hd].astype(BF16)
    w_f = jnp.pad(w_in[:, 3 * hd:], ((0, 0), (0, LANES - heads))).astype(BF16)
    b_pad = jnp.pad(b_f, (0, LANES - heads)).reshape(1, LANES).astype(F32)
    qkv, logf = _rms_proj(x.reshape(B * S, D), gain, w_qkv, tm=tm, tn=min(1024, hd),
                          out_dtype=BF16, q_cols=hd, q_scale=HEAD_DIM ** -0.5 * LOG2E,
                          gate=(w_f, b_pad))
    ccol, crow = _fox_cumsum(logf.reshape(B, S, LANES), heads)
    o = _fox_attention(qkv.reshape(B, S, 3 * hd), ccol, crow, heads=heads, tile=attn_tile)
    y = _proj_res([o.reshape(B * S, hd)], w_out.astype(BF16), x.reshape(B * S, D), tm=tm,
                  tn=min(1024, D))
    return y.reshape(B, S, D)


def _dilated_mixer(x, gain, w_in, w_out, rel_bias, *, patterns, group_heads, tm):
    B, S, D = x.shape
    n_g = len(patterns)
    hd = n_g * group_heads * HEAD_DIM
    qkv = _rms_proj(x.reshape(B * S, D), gain, w_in.astype(BF16), tm=tm,
                    tn=group_heads * HEAD_DIM, out_dtype=F32, q_cols=hd, q_scale=HEAD_DIM ** -0.5)
    tables = _dil_bias_tables(rel_bias, patterns, group_heads)
    outs = _dil_attention(qkv.reshape(B, S, 3 * hd), tables, patterns=patterns,
                          group_heads=group_heads)
    outs = [o.reshape(B * S, group_heads * HEAD_DIM) for o in outs]
    y = _proj_res(outs, w_out.astype(BF16), x.reshape(B * S, D), tm=tm, tn=min(1024, D))
    return y.reshape(B, S, D)


def _forward(x, mem, ffn1_norm, ffn1_w_in, ffn1_w_out, mix_norm, fox_w_in, fox_b_f, fox_w_out,
             dil_w_in, dil_w_out, rel_bias, cross_norm, mem_norm, cross_w_q, cross_w_kv,
             cross_w_out, ffn2_norm, ffn2_w_in, ffn2_w_out, final_norm, *,
             patterns, fox_heads, group_heads, cross_heads, tm, tf, proj_tm, attn_tile, cross_tm):
    B, S, D = x.shape
    M = mem.shape[1]
    depth = ffn1_norm.shape[0]
    n_mixers = 2
    for i in range(depth):
        x = _ffn(x.reshape(B * S, D), ffn1_norm[i], ffn1_w_in[i].astype(BF16),
                 ffn1_w_out[i].astype(BF16), tm=tm, tf=tf).reshape(B, S, D)
        j = i // n_mixers
        if i % n_mixers == 0:
            x = _fox_mixer(x, mix_norm[i], fox_w_in[j], fox_b_f[j], fox_w_out[j],
                           heads=fox_heads, attn_tile=attn_tile, tm=proj_tm)
        else:
            x = _dilated_mixer(x, mix_norm[i], dil_w_in[j], dil_w_out[j], rel_bias,
                               patterns=patterns, group_heads=group_heads, tm=proj_tm)
        kv = _rms_proj(mem.reshape(B * M, D), mem_norm, cross_w_kv[i].astype(BF16),
                       tm=min(tm, B * M), tn=min(512, cross_heads * HEAD_DIM), out_dtype=BF16)
        x = _cross_attention(x, cross_norm[i], cross_w_q[i].astype(BF16),
                             kv.reshape(B, M, -1), cross_w_out[i].astype(BF16),
                             heads=cross_heads, tm=cross_tm)
        x = _ffn(x.reshape(B * S, D), ffn2_norm[i], ffn2_w_in[i].astype(BF16),
                 ffn2_w_out[i].astype(BF16), tm=tm, tf=tf,
                 final_gain=final_norm if i == depth - 1 else None).reshape(B, S, D)
    return x


def kernel(x, mem, ffn1_norm, ffn1_w_in, ffn1_w_out, mix_norm, fox_w_in, fox_b_f, fox_w_out, dil_w_in, dil_w_out, rel_bias, cross_norm, mem_norm, cross_w_q, cross_w_kv, cross_w_out, ffn2_norm, ffn2_w_in, ffn2_w_out, final_norm):
    return _forward(x, mem, ffn1_norm, ffn1_w_in, ffn1_w_out, mix_norm, fox_w_in, fox_b_f,
                    fox_w_out, dil_w_in, dil_w_out, rel_bias, cross_norm, mem_norm, cross_w_q,
                    cross_w_kv, cross_w_out, ffn2_norm, ffn2_w_in, ffn2_w_out, final_norm,
                    patterns=DIL_PATTERNS, fox_heads=FOX_HEADS, group_heads=DIL_GROUP_HEADS,
                    cross_heads=CROSS_HEADS, tm=512, tf=512, proj_tm=1024, attn_tile=512,
                    cross_tm=512)
```

```python
import functools

import numpy as np
import jax
import jax.numpy as jnp
from jax import lax
from jax.experimental import pallas as pl
from jax.experimental.pallas import tpu as pltpu

HEAD_DIM = 128
FOX_HEADS = 16
DIL_PATTERNS = ((128, 1), (512, 4), (2048, 16))
DIL_GROUP_HEADS = 6
CROSS_HEADS = 4
NUM_BUCKETS = 32
MAX_DISTANCE = 2048
RMS_EPS = 1e-6
NEG_INF = -1e30

LANES = 128
DIL_TILE = 128
VMEM_LIMIT_BYTES = 56 * 1024 * 1024

BF16 = jnp.bfloat16
F32 = jnp.float32
_NT = (((1,), (1,)), ((), ()))
LOG2E = 1.4426950408889634


def _params():
    return pltpu.CompilerParams(vmem_limit_bytes=VMEM_LIMIT_BYTES)


def _rmsnorm(x, g):
    return x * lax.rsqrt(jnp.mean(x * x, axis=-1, keepdims=True) + RMS_EPS) * g


def _log_sigmoid(z):
    return jnp.minimum(z, 0.0) - jnp.log1p(jnp.exp(-jnp.abs(z)))


def _split3(x):
    hi = x.astype(BF16)
    r = x - hi.astype(F32)
    mid = r.astype(BF16)
    lo = (r - mid.astype(F32)).astype(BF16)
    return hi, mid, lo


def _rms_proj_kernel(x_ref, g_ref, w_ref, *rest, tn, q_cols, q_scale, with_gate):
    if with_gate:
        wf_ref, bf_ref, o_ref, logf_ref, h_ref = rest
    else:
        o_ref, h_ref = rest
    j = pl.program_id(1)

    @pl.when(j == 0)
    def _():
        h = _rmsnorm(x_ref[...], g_ref[...]).astype(BF16)
        h_ref[...] = h
        if with_gate:
            z = jnp.dot(h, wf_ref[...], preferred_element_type=F32) + bf_ref[...]
            logf_ref[...] = _log_sigmoid(z)

    acc = jnp.dot(h_ref[...], w_ref[...], preferred_element_type=F32)
    if q_cols:
        acc = acc * jnp.where(j * tn < q_cols, q_scale, 1.0).astype(F32)
    o_ref[...] = acc.astype(o_ref.dtype)


def _rms_proj(x, gain, w, layer, *, n_out, tm, tn, out_dtype, q_cols=0, q_scale=1.0, gate=None):
    T, D = x.shape
    assert T % tm == 0 and n_out % tn == 0 and q_cols % tn == 0 and n_out <= w.shape[2]
    in_specs = [
        pl.BlockSpec((tm, D), lambda i, j: (i, 0)),
        pl.BlockSpec((1, D), lambda i, j: (0, 0)),
        pl.BlockSpec((None, D, tn), lambda i, j: (layer, 0, j)),
    ]
    args = [x, gain.reshape(1, D), w]
    out_shape = [jax.ShapeDtypeStruct((T, n_out), out_dtype)]
    out_specs = [pl.BlockSpec((tm, tn), lambda i, j: (i, j))]
    if gate is not None:
        in_specs += [pl.BlockSpec((D, LANES), lambda i, j: (0, 0)),
                     pl.BlockSpec((1, LANES), lambda i, j: (0, 0))]
        args += list(gate)
        out_shape.append(jax.ShapeDtypeStruct((T, LANES), F32))
        out_specs.append(pl.BlockSpec((tm, LANES), lambda i, j: (i, 0)))
    res = pl.pallas_call(
        functools.partial(_rms_proj_kernel, tn=tn, q_cols=q_cols, q_scale=q_scale,
                          with_gate=gate is not None),
        grid=(T // tm, n_out // tn),
        in_specs=in_specs,
        out_specs=out_specs,
        out_shape=out_shape,
        scratch_shapes=[pltpu.VMEM((tm, D), BF16)],
        compiler_params=_params(),
        name="rms_proj_gate" if gate is not None else "rms_proj",
    )(*args)
    return res if gate is not None else res[0]


def _ffn_kernel(x_ref, g_ref, wg_ref, wu_ref, wo_ref, *rest, final_norm):
    if final_norm:
        fg_ref, o_ref, h_ref = rest
    else:
        o_ref, h_ref = rest
    j = pl.program_id(1)

    @pl.when(j == 0)
    def _():
        x = x_ref[...]
        h_ref[...] = _rmsnorm(x, g_ref[...]).astype(BF16)
        o_ref[...] = x

    h = h_ref[...]
    gate = jnp.dot(h, wg_ref[...], preferred_element_type=F32)
    up = jnp.dot(h, wu_ref[...], preferred_element_type=F32)
    act = (gate * jax.nn.sigmoid(gate) * (0.5 * up)).astype(BF16)
    o_ref[...] += jnp.dot(act, wo_ref[...], preferred_element_type=F32)

    if final_norm:
        @pl.when(j == pl.num_programs(1) - 1)
        def _():
            o_ref[...] = _rmsnorm(o_ref[...], fg_ref[...])


def _ffn(x, gain, w_in, w_out, layer, *, tm, tf, final_gain=None):
    T, D = x.shape
    F = w_out.shape[1]
    assert T % tm == 0 and F % tf == 0
    nf = F // tf
    in_specs = [
        pl.BlockSpec((tm, D), lambda i, j: (i, 0)),
        pl.BlockSpec((1, D), lambda i, j: (0, 0)),
        pl.BlockSpec((None, D, tf), lambda i, j: (layer, 0, j)),
        pl.BlockSpec((None, D, tf), lambda i, j: (layer, 0, j + nf)),
        pl.BlockSpec((None, tf, D), lambda i, j: (layer, j, 0)),
    ]
    args = [x, gain.reshape(1, D), w_in, w_in, w_out]
    if final_gain is not None:
        in_specs.append(pl.BlockSpec((1, D), lambda i, j: (0, 0)))
        args.append(final_gain.reshape(1, D))
    return pl.pallas_call(
        functools.partial(_ffn_kernel, final_norm=final_gain is not None),
        grid=(T // tm, nf),
        in_specs=in_specs,
        out_specs=pl.BlockSpec((tm, D), lambda i, j: (i, 0)),
        out_shape=jax.ShapeDtypeStruct((T, D), F32),
        scratch_shapes=[pltpu.VMEM((tm, D), BF16)],
        compiler_params=_params(),
        name="ffn_final" if final_gain is not None else "ffn",
    )(*args)


def _proj_res_kernel(*refs, n_a):
    a_refs, w_refs = refs[:n_a], refs[n_a:2 * n_a]
    x_ref, o_ref = refs[2 * n_a:]
    acc = x_ref[...]
    for a_ref, w_ref in zip(a_refs, w_refs):
        acc = acc + jnp.dot(a_ref[...], w_ref[...], preferred_element_type=F32)
    o_ref[...] = acc


def _proj_res(a_list, w, layer, x, *, tm, tn):
    T, N = x.shape
    kg = a_list[0].shape[1]
    assert all(a.shape == (T, kg) for a in a_list) and kg * len(a_list) == w.shape[1]
    assert T % tm == 0 and N % tn == 0
    n_a = len(a_list)
    in_specs = [pl.BlockSpec((tm, kg), lambda i, j: (i, 0)) for _ in a_list]
    in_specs += [pl.BlockSpec((None, kg, tn), lambda i, j, g=g: (layer, g, j)) for g in range(n_a)]
    in_specs.append(pl.BlockSpec((tm, tn), lambda i, j: (i, j)))
    return pl.pallas_call(
        functools.partial(_proj_res_kernel, n_a=n_a),
        grid=(T // tm, N // tn),
        in_specs=in_specs,
        out_specs=pl.BlockSpec((tm, tn), lambda i, j: (i, j)),
        out_shape=jax.ShapeDtypeStruct((T, N), F32),
        compiler_params=_params(),
        name="proj_res",
    )(*a_list, *([w] * n_a), x)


def _cumsum_kernel(tri_ref, x_ref, ccol_ref, crow_ref, *, heads):
    pieces = _split3(x_ref[0] * LOG2E)
    tri = tri_ref[...]
    c = sum(jnp.dot(tri, p, preferred_element_type=F32) for p in pieces)
    ccol_ref[0] = c
    crow_ref[0] = c.T[:heads]


def _fox_cumsum(logf, heads):
    B, S, _ = logf.shape
    tri = jnp.tril(jnp.ones((S, S), BF16))
    return pl.pallas_call(
        functools.partial(_cumsum_kernel, heads=heads),
        grid=(B,),
        in_specs=[pl.BlockSpec((S, S), lambda b: (0, 0)),
                  pl.BlockSpec((1, S, LANES), lambda b: (b, 0, 0))],
        out_specs=[pl.BlockSpec((1, S, LANES), lambda b: (b, 0, 0)),
                   pl.BlockSpec((1, heads, S), lambda b: (b, 0, 0))],
        out_shape=[jax.ShapeDtypeStruct((B, S, LANES), F32),
                   jax.ShapeDtypeStruct((B, heads, S), F32)],
        compiler_params=_params(),
        name="fox_cumsum",
    )(tri, logf)


def _fox_attn_kernel(q_ref, k_ref, v_ref, ccol_ref, crow_ref, o_ref, *, tile, hpb):
    hb = pl.program_id(1)
    qi = pl.program_id(2)
    lane = lax.broadcasted_iota(jnp.int32, (tile, LANES), 1)
    ccol = ccol_ref[0]
    heads = range(hpb)
    q = [q_ref[0, :, e * HEAD_DIM:(e + 1) * HEAD_DIM] for e in heads]
    cq = [jnp.sum(jnp.where(lane == hb * hpb + e, ccol, 0.0), axis=1, keepdims=True)
          for e in heads]

    def step(j, carry, diagonal):
        start = pl.multiple_of(j * tile, tile)
        out = []
        for e in heads:
            m, l, acc = carry[e]
            k = k_ref[0, pl.ds(start, tile), e * HEAD_DIM:(e + 1) * HEAD_DIM]
            v = v_ref[0, pl.ds(start, tile), e * HEAD_DIM:(e + 1) * HEAD_DIM]
            ck = crow_ref[0, e, pl.ds(j, 1), :]
            s = lax.dot_general(q[e], k, _NT, preferred_element_type=F32) - ck
            if diagonal:
                row = lax.broadcasted_iota(jnp.int32, (tile, tile), 0)
                col = lax.broadcasted_iota(jnp.int32, (tile, tile), 1)
                s = jnp.where(col <= row, s, NEG_INF)
            m_new = jnp.maximum(m, jnp.max(s, axis=1, keepdims=True) + cq[e])
            alpha = jnp.exp2(m - m_new)
            p = jnp.exp2(s - (m_new - cq[e]))
            l = alpha * l + jnp.sum(p, axis=1, keepdims=True)
            acc = alpha * acc + jnp.dot(p.astype(BF16), v, preferred_element_type=F32)
            out.append((m_new, l, acc))
        return tuple(out)

    init = tuple((jnp.full((tile, 1), NEG_INF, F32), jnp.zeros((tile, 1), F32),
                  jnp.zeros((tile, HEAD_DIM), F32)) for _ in heads)
    carry = lax.fori_loop(0, qi, lambda j, c: step(j, c, False), init)
    carry = step(qi, carry, True)
    for e in heads:
        _, l, acc = carry[e]
        o_ref[0, :, e * HEAD_DIM:(e + 1) * HEAD_DIM] = (acc / l).astype(o_ref.dtype)


def _fox_attention(qkv, ccol, crow, *, heads, tile, hpb=2):
    B, S, _ = qkv.shape
    assert S % tile == 0 and heads % hpb == 0
    nq = S // tile
    nhb = heads // hpb
    w = hpb * HEAD_DIM
    crow4 = crow.reshape(B, heads, nq, tile)
    return pl.pallas_call(
        functools.partial(_fox_attn_kernel, tile=tile, hpb=hpb),
        grid=(B, nhb, nq),
        in_specs=[
            pl.BlockSpec((1, tile, w), lambda b, h, i: (b, i, h)),
            pl.BlockSpec((1, S, w), lambda b, h, i: (b, 0, nhb + h)),
            pl.BlockSpec((1, S, w), lambda b, h, i: (b, 0, 2 * nhb + h)),
            pl.BlockSpec((1, tile, LANES), lambda b, h, i: (b, i, 0)),
            pl.BlockSpec((1, hpb, nq, tile), lambda b, h, i: (b, h, 0, 0)),
        ],
        out_specs=pl.BlockSpec((1, tile, w), lambda b, h, i: (b, i, h)),
        out_shape=jax.ShapeDtypeStruct((B, S, heads * HEAD_DIM), BF16),
        compiler_params=_params(),
        name="fox_attn",
    )(qkv, qkv, qkv, ccol, crow4)


def _t5_bucket(dist):
    max_exact = NUM_BUCKETS // 2
    d = np.maximum(dist, 1).astype(np.float32)
    large = max_exact + (np.log(d / max_exact) / np.log(MAX_DISTANCE / max_exact)
                         * (NUM_BUCKETS - max_exact)).astype(np.int32)
    large = np.minimum(large, NUM_BUCKETS - 1)
    return np.where(dist < max_exact, dist, large).astype(np.int32)


def _dil_bias_tables(rel_bias, patterns, group_heads):
    period = 4 * DIL_TILE
    c = np.arange(period)
    back = c < 2 * DIL_TILE
    groups = []
    for g, (window, dil) in enumerate(patterns):
        assert window // dil == DIL_TILE
        strips = []
        for t in (0, 1):
            delta = np.where(back, t * DIL_TILE - c, t * DIL_TILE + period - c)
            valid = (delta >= 0) & (delta <= DIL_TILE) & (back | (c > period - DIL_TILE))
            bucket = _t5_bucket(np.clip(delta, 0, DIL_TILE) * dil)
            vec = rel_bias[bucket][:, g * group_heads:(g + 1) * group_heads].astype(F32)
            vec = jnp.where(valid[:, None], vec, NEG_INF).T
            flat = jnp.tile(vec, (1, DIL_TILE))[:, :DIL_TILE * (period - 1)]
            strips.append(flat.reshape(group_heads, DIL_TILE, period - 1)[:, :, :2 * DIL_TILE])
        groups.append(jnp.stack(strips, axis=1))
    return jnp.stack(groups)


def _dil_attn_kernel(*refs, dils, seq):
    n_g = len(dils)
    q_refs, k_refs, v_refs = refs[:n_g], refs[n_g:2 * n_g], refs[2 * n_g:3 * n_g]
    t_ref = refs[3 * n_g]
    o_refs = refs[3 * n_g + 1:4 * n_g + 1]
    o_sc, lse_sc = refs[4 * n_g + 1:]

    def rows(start, size, stride):
        return pl.ds(start, size) if stride == 1 else pl.ds(start, size, stride=stride)

    for g, d in enumerate(dils):
        cls_len = seq // d
        n_keys = min(2 * DIL_TILE, cls_len)
        for r in range(d):
            for u0 in range(0, cls_len, DIL_TILE):
                s0 = min(max(u0 - DIL_TILE, 0), cls_len - n_keys)
                strip = (u0 - s0) // DIL_TILE
                q_rows = rows(r + d * u0, DIL_TILE, d)
                k_rows = rows(r + d * s0, n_keys, d)
                q = q_refs[g][0, q_rows, :].astype(BF16)
                k = k_refs[g][0, k_rows, :].astype(BF16)
                v = v_refs[g][0, k_rows, :].astype(BF16)
                s = lax.dot_general(q, k, _NT, preferred_element_type=F32)
                s = s + t_ref[g, 0, strip][:, :n_keys]
                m = jnp.max(s, axis=1, keepdims=True)
                p = jnp.exp(s - m)
                l = jnp.sum(p, axis=1, keepdims=True)
                o = jnp.dot(p.astype(BF16), v, preferred_element_type=F32) / l
                o_sc[g, q_rows, :] = o
                lse_sc[g, q_rows, :] = jnp.broadcast_to(m + jnp.log(l), (DIL_TILE, LANES))

    lse = [lse_sc[g] for g in range(n_g)]
    mx = functools.reduce(jnp.maximum, lse)
    e = [jnp.exp(x - mx) for x in lse]
    inv = 1.0 / functools.reduce(lambda x, y: x + y, e)
    for g in range(n_g):
        o_refs[g][0] = (o_sc[g] * (e[g] * inv)).astype(o_refs[g].dtype)


def _dil_attention(qkv, tables, *, patterns, group_heads):
    B, S, _ = qkv.shape
    n_g = len(patterns)
    dils = tuple(d for _, d in patterns)
    assert all(S % (d * DIL_TILE) == 0 for d in dils)
    nh = n_g * group_heads

    def head_spec(part, g):
        return pl.BlockSpec((1, S, HEAD_DIM),
                            lambda b, h, part=part, g=g: (b, 0, part * nh + g * group_heads + h))

    in_specs = [head_spec(part, g) for part in range(3) for g in range(n_g)]
    in_specs.append(pl.BlockSpec((n_g, 1, 2, DIL_TILE, 2 * DIL_TILE),
                                 lambda b, h: (0, h, 0, 0, 0)))
    out_spec = pl.BlockSpec((1, S, HEAD_DIM), lambda b, h: (b, 0, h))
    return pl.pallas_call(
        functools.partial(_dil_attn_kernel, dils=dils, seq=S),
        grid=(B, group_heads),
        in_specs=in_specs,
        out_specs=[out_spec] * n_g,
        out_shape=[jax.ShapeDtypeStruct((B, S, group_heads * HEAD_DIM), BF16)] * n_g,
        scratch_shapes=[pltpu.VMEM((n_g, S, HEAD_DIM), F32), pltpu.VMEM((n_g, S, LANES), F32)],
        compiler_params=_params(),
        name="dil_attn",
    )(*([qkv] * (3 * n_g)), tables)


def _cross_kernel(x_ref, g_ref, wq_ref, kv_ref, wo_ref, o_ref, *, heads, scale):
    x = x_ref[0]
    h = _rmsnorm(x, g_ref[...]).astype(BF16)
    q = (jnp.dot(h, wq_ref[...], preferred_element_type=F32) * scale).astype(BF16)
    kv = kv_ref[0]
    outs = []
    for hd in range(heads):
        qh = q[:, hd * HEAD_DIM:(hd + 1) * HEAD_DIM]
        kh = kv[:, hd * HEAD_DIM:(hd + 1) * HEAD_DIM]
        vh = kv[:, (heads + hd) * HEAD_DIM:(heads + hd + 1) * HEAD_DIM]
        s = lax.dot_general(qh, kh, _NT, preferred_element_type=F32)
        p = jnp.exp(s - jnp.max(s, axis=1, keepdims=True))
        l = jnp.sum(p, axis=1, keepdims=True)
        outs.append((jnp.dot(p.astype(BF16), vh, preferred_element_type=F32) / l).astype(BF16))
    o = jnp.concatenate(outs, axis=1)
    o_ref[0] = x + jnp.dot(o, wo_ref[...], preferred_element_type=F32)


def _cross_attention(x, gain, wq, kv, wo, layer, *, heads, tm):
    B, S, D = x.shape
    M = kv.shape[1]
    hd = heads * HEAD_DIM
    assert S % tm == 0
    return pl.pallas_call(
        functools.partial(_cross_kernel, heads=heads, scale=HEAD_DIM ** -0.5),
        grid=(B, S // tm),
        in_specs=[
            pl.BlockSpec((1, tm, D), lambda b, i: (b, i, 0)),
            pl.BlockSpec((1, D), lambda b, i: (0, 0)),
            pl.BlockSpec((None, D, hd), lambda b, i: (layer, 0, 0)),
            pl.BlockSpec((1, M, 2 * hd), lambda b, i: (b, 0, 0)),
            pl.BlockSpec((None, hd, D), lambda b, i: (layer, 0, 0)),
        ],
        out_specs=pl.BlockSpec((1, tm, D), lambda b, i: (b, i, 0)),
        out_shape=jax.ShapeDtypeStruct((B, S, D), F32),
        compiler_params=_params(),
        name="cross_attn",
    )(x, gain.reshape(1, D), wq, kv, wo)


def _fox_mixer(x, gain, w_in, w_gate, b_f, w_out, layer, *, heads, attn_tile, tm):
    B, S, D = x.shape
    hd = heads * HEAD_DIM
    w_f = jnp.pad(w_gate, ((0, 0), (0, LANES - heads))).astype(BF16)
    b_pad = jnp.pad(b_f, (0, LANES - heads)).reshape(1, LANES).astype(F32)
    qkv, logf = _rms_proj(x.reshape(B * S, D), gain, w_in, layer, n_out=3 * hd, tm=tm,
                          tn=min(1024, hd), out_dtype=BF16, q_cols=hd,
                          q_scale=HEAD_DIM ** -0.5 * LOG2E, gate=(w_f, b_pad))
    ccol, crow = _fox_cumsum(logf.reshape(B, S, LANES), heads)
    o = _fox_attention(qkv.reshape(B, S, 3 * hd), ccol, crow, heads=heads, tile=attn_tile)
    y = _proj_res([o.reshape(B * S, hd)], w_out, layer, x.reshape(B * S, D), tm=tm,
                  tn=min(1024, D))
    return y.reshape(B, S, D)


def _dilated_mixer(x, gain, w_in, w_out, layer, rel_bias, *, patterns, group_heads, tm):
    B, S, D = x.shape
    n_g = len(patterns)
    hd = n_g * group_heads * HEAD_DIM
    qkv = _rms_proj(x.reshape(B * S, D), gain, w_in, layer, n_out=3 * hd, tm=tm,
                    tn=group_heads * HEAD_DIM, out_dtype=F32, q_cols=hd, q_scale=HEAD_DIM ** -0.5)
    tables = _dil_bias_tables(rel_bias, patterns, group_heads)
    outs = _dil_attention(qkv.reshape(B, S, 3 * hd), tables, patterns=patterns,
                          group_heads=group_heads)
    outs = [o.reshape(B * S, group_heads * HEAD_DIM) for o in outs]
    y = _proj_res(outs, w_out, layer, x.reshape(B * S, D), tm=tm, tn=min(1024, D))
    return y.reshape(B, S, D)


def _forward(x, mem, ffn1_norm, ffn1_w_in, ffn1_w_out, mix_norm, fox_w_in, fox_b_f, fox_w_out,
             dil_w_in, dil_w_out, rel_bias, cross_norm, mem_norm, cross_w_q, cross_w_kv,
             cross_w_out, ffn2_norm, ffn2_w_in, ffn2_w_out, final_norm, *,
             patterns, fox_heads, group_heads, cross_heads, tm, tf, proj_tm, attn_tile, cross_tm):
    B, S, D = x.shape
    M = mem.shape[1]
    depth = ffn1_norm.shape[0]
    n_mixers = 2
    fox_hd = fox_heads * HEAD_DIM
    cross_hd = cross_heads * HEAD_DIM
    fox_w_gate = fox_w_in[:, :, 3 * fox_hd:]
    (ffn1_w_in, ffn1_w_out, ffn2_w_in, ffn2_w_out, fox_w_in, fox_w_out, dil_w_in, dil_w_out,
     cross_w_q, cross_w_kv, cross_w_out) = (
        w.astype(BF16) for w in (ffn1_w_in, ffn1_w_out, ffn2_w_in, ffn2_w_out, fox_w_in,
                                 fox_w_out, dil_w_in, dil_w_out, cross_w_q, cross_w_kv,
                                 cross_w_out))
    for i in range(depth):
        x = _ffn(x.reshape(B * S, D), ffn1_norm[i], ffn1_w_in, ffn1_w_out, i,
                 tm=tm, tf=tf).reshape(B, S, D)
        j = i // n_mixers
        if i % n_mixers == 0:
            x = _fox_mixer(x, mix_norm[i], fox_w_in, fox_w_gate[j], fox_b_f[j], fox_w_out, j,
                           heads=fox_heads, attn_tile=attn_tile, tm=proj_tm)
        else:
            x = _dilated_mixer(x, mix_norm[i], dil_w_in, dil_w_out, j, rel_bias,
                               patterns=patterns, group_heads=group_heads, tm=proj_tm)
        kv = _rms_proj(mem.reshape(B * M, D), mem_norm, cross_w_kv, i, n_out=2 * cross_hd,
                       tm=min(tm, B * M), tn=min(512, cross_hd), out_dtype=BF16)
        x = _cross_attention(x, cross_norm[i], cross_w_q, kv.reshape(B, M, 2 * cross_hd),
                             cross_w_out, i, heads=cross_heads, tm=cross_tm)
        x = _ffn(x.reshape(B * S, D), ffn2_norm[i], ffn2_w_in, ffn2_w_out, i, tm=tm, tf=tf,
                 final_gain=final_norm if i == depth - 1 else None).reshape(B, S, D)
    return x


def kernel(x, mem, ffn1_norm, ffn1_w_in, ffn1_w_out, mix_norm, fox_w_in, fox_b_f, fox_w_out, dil_w_in, dil_w_out, rel_bias, cross_norm, mem_norm, cross_w_q, cross_w_kv, cross_w_out, ffn2_norm, ffn2_w_in, ffn2_w_out, final_norm):
    return _forward(x, mem, ffn1_norm, ffn1_w_in, ffn1_w_out, mix_norm, fox_w_in, fox_b_f,
                    fox_w_out, dil_w_in, dil_w_out, rel_bias, cross_norm, mem_norm, cross_w_q,
                    cross_w_kv, cross_w_out, ffn2_norm, ffn2_w_in, ffn2_w_out, final_norm,
                    patterns=DIL_PATTERNS, fox_heads=FOX_HEADS, group_heads=DIL_GROUP_HEADS,
                    cross_heads=CROSS_HEADS, tm=512, tf=512, proj_tm=1024, attn_tile=512,
                    cross_tm=512)
```

```python
import functools

import numpy as np
import jax
import jax.numpy as jnp
from jax import lax
from jax.experimental import pallas as pl
from jax.experimental.pallas import tpu as pltpu

HEAD_DIM = 128
FOX_HEADS = 16
DIL_PATTERNS = ((128, 1), (512, 4), (2048, 16))
DIL_GROUP_HEADS = 6
CROSS_HEADS = 4
NUM_BUCKETS = 32
MAX_DISTANCE = 2048
RMS_EPS = 1e-6
NEG_INF = -1e30

LANES = 128
DIL_TILE = 128
VMEM_LIMIT_BYTES = 56 * 1024 * 1024

BF16 = jnp.bfloat16
F32 = jnp.float32
_NT = (((1,), (1,)), ((), ()))
LOG2E = 1.4426950408889634


def _params():
    return pltpu.CompilerParams(vmem_limit_bytes=VMEM_LIMIT_BYTES)


def _rmsnorm(x, g):
    return x * lax.rsqrt(jnp.mean(x * x, axis=-1, keepdims=True) + RMS_EPS) * g


def _log_sigmoid(z):
    return jnp.minimum(z, 0.0) - jnp.log1p(jnp.exp(-jnp.abs(z)))


def _split3(x):
    hi = x.astype(BF16)
    r = x - hi.astype(F32)
    mid = r.astype(BF16)
    lo = (r - mid.astype(F32)).astype(BF16)
    return hi, mid, lo


def _rms_proj_kernel(x_ref, g_ref, w_ref, *rest, tn, q_cols, q_scale, with_gate):
    if with_gate:
        wf_ref, bf_ref, o_ref, logf_ref, h_ref = rest
    else:
        o_ref, h_ref = rest
    j = pl.program_id(1)

    @pl.when(j == 0)
    def _():
        h = _rmsnorm(x_ref[...], g_ref[...]).astype(BF16)
        h_ref[...] = h
        if with_gate:
            z = jnp.dot(h, wf_ref[...], preferred_element_type=F32) + bf_ref[...]
            logf_ref[...] = _log_sigmoid(z)

    acc = jnp.dot(h_ref[...], w_ref[...].astype(BF16), preferred_element_type=F32)
    if q_cols:
        acc = acc * jnp.where(j * tn < q_cols, q_scale, 1.0).astype(F32)
    o_ref[...] = acc.astype(o_ref.dtype)


def _rms_proj(x, gain, w, layer, *, n_out, tm, tn, out_dtype, q_cols=0, q_scale=1.0, gate=None):
    T, D = x.shape
    assert T % tm == 0 and n_out % tn == 0 and q_cols % tn == 0 and n_out <= w.shape[2]
    in_specs = [
        pl.BlockSpec((tm, D), lambda i, j: (i, 0)),
        pl.BlockSpec((1, D), lambda i, j: (0, 0)),
        pl.BlockSpec((None, D, tn), lambda i, j: (layer, 0, j)),
    ]
    args = [x, gain.reshape(1, D), w]
    out_shape = [jax.ShapeDtypeStruct((T, n_out), out_dtype)]
    out_specs = [pl.BlockSpec((tm, tn), lambda i, j: (i, j))]
    if gate is not None:
        in_specs += [pl.BlockSpec((D, LANES), lambda i, j: (0, 0)),
                     pl.BlockSpec((1, LANES), lambda i, j: (0, 0))]
        args += list(gate)
        out_shape.append(jax.ShapeDtypeStruct((T, LANES), F32))
        out_specs.append(pl.BlockSpec((tm, LANES), lambda i, j: (i, 0)))
    res = pl.pallas_call(
        functools.partial(_rms_proj_kernel, tn=tn, q_cols=q_cols, q_scale=q_scale,
                          with_gate=gate is not None),
        grid=(T // tm, n_out // tn),
        in_specs=in_specs,
        out_specs=out_specs,
        out_shape=out_shape,
        scratch_shapes=[pltpu.VMEM((tm, D), BF16)],
        compiler_params=_params(),
        name="rms_proj_gate" if gate is not None else "rms_proj",
    )(*args)
    return res if gate is not None else res[0]


def _ffn_start(x_ref, g_ref, h_ref, o_ref):
    x = x_ref[...]
    h_ref[...] = _rmsnorm(x, g_ref[...]).astype(BF16)
    o_ref[...] = x


def _ffn_chunk(h_ref, o_ref, wg, wu, wo):
    h = h_ref[...]
    gate = jnp.dot(h, wg, preferred_element_type=F32)
    up = jnp.dot(h, wu, preferred_element_type=F32)
    act = (gate * jax.nn.sigmoid(gate) * (0.5 * up)).astype(BF16)
    o_ref[...] += jnp.dot(act, wo, preferred_element_type=F32)


def _ffn_head_kernel(x_ref, g_ref, wg_ref, wu_ref, wo_ref, *rest, final_norm):
    if final_norm:
        fg_ref, o_ref, wg_out, wu_out, wo_out, h_ref = rest
    else:
        o_ref, wg_out, wu_out, wo_out, h_ref = rest
    j = pl.program_id(0)

    @pl.when(j == 0)
    def _():
        _ffn_start(x_ref, g_ref, h_ref, o_ref)

    wg, wu, wo = (w[...].astype(BF16) for w in (wg_ref, wu_ref, wo_ref))
    wg_out[...] = wg
    wu_out[...] = wu
    wo_out[...] = wo
    _ffn_chunk(h_ref, o_ref, wg, wu, wo)

    if final_norm:
        @pl.when(j == pl.num_programs(0) - 1)
        def _():
            o_ref[...] = _rmsnorm(o_ref[...], fg_ref[...])


def _ffn_main_kernel(x_ref, g_ref, wg_ref, wu_ref, wo_ref, head_ref, *rest, final_norm, n_head):
    if final_norm:
        fg_ref, o_ref, h_ref = rest
    else:
        o_ref, h_ref = rest
    i = pl.program_id(0)
    j = pl.program_id(1)
    last = pl.num_programs(1) - 1

    @pl.when(jnp.logical_and(i < n_head, j == last))
    def _():
        o_ref[...] = head_ref[...]

    @pl.when(i >= n_head)
    def _():
        @pl.when(j == 0)
        def _():
            _ffn_start(x_ref, g_ref, h_ref, o_ref)

        _ffn_chunk(h_ref, o_ref, wg_ref[...], wu_ref[...], wo_ref[...])

        if final_norm:
            @pl.when(j == last)
            def _():
                o_ref[...] = _rmsnorm(o_ref[...], fg_ref[...])


def _ffn(x, gain, w_in, w_out, layer, *, tm, tf, head_tm, head_tf, final_gain=None):
    T, D = x.shape
    F = w_out.shape[1]
    assert T % tm == 0 and F % tf == 0 and F % head_tf == 0
    assert head_tm % tm == 0 and head_tm <= T
    final = final_gain is not None
    gain2 = gain.reshape(1, D)
    extra_specs = [pl.BlockSpec((1, D), lambda *_: (0, 0))] if final else []
    extra_args = [final_gain.reshape(1, D)] if final else []

    nfh = F // head_tf
    once = dict(pipeline_mode=pl.Buffered(1))
    head, wg, wu, wo = pl.pallas_call(
        functools.partial(_ffn_head_kernel, final_norm=final),
        grid=(nfh,),
        in_specs=[
            pl.BlockSpec((head_tm, D), lambda j: (0, 0), **once),
            pl.BlockSpec((1, D), lambda j: (0, 0)),
            pl.BlockSpec((None, D, head_tf), lambda j: (layer, 0, j)),
            pl.BlockSpec((None, D, head_tf), lambda j: (layer, 0, j + nfh)),
            pl.BlockSpec((None, head_tf, D), lambda j: (layer, j, 0)),
        ] + extra_specs,
        out_specs=[
            pl.BlockSpec((head_tm, D), lambda j: (0, 0), **once),
            pl.BlockSpec((D, head_tf), lambda j: (0, j)),
            pl.BlockSpec((D, head_tf), lambda j: (0, j)),
            pl.BlockSpec((head_tf, D), lambda j: (j, 0)),
        ],
        out_shape=[
            jax.ShapeDtypeStruct((head_tm, D), F32),
            jax.ShapeDtypeStruct((D, F), BF16),
            jax.ShapeDtypeStruct((D, F), BF16),
            jax.ShapeDtypeStruct((F, D), BF16),
        ],
        scratch_shapes=[pltpu.VMEM((head_tm, D), BF16)],
        compiler_params=_params(),
        name="ffn_head_final" if final else "ffn_head",
    )(x, gain2, w_in, w_in, w_out, *extra_args)

    n_head = head_tm // tm

    def chunk(i, j):
        return jnp.where(i < n_head, 0, j)

    return pl.pallas_call(
        functools.partial(_ffn_main_kernel, final_norm=final, n_head=n_head),
        grid=(T // tm, F // tf),
        in_specs=[
            pl.BlockSpec((tm, D), lambda i, j: (i, 0)),
            pl.BlockSpec((1, D), lambda i, j: (0, 0)),
            pl.BlockSpec((D, tf), lambda i, j: (0, chunk(i, j))),
            pl.BlockSpec((D, tf), lambda i, j: (0, chunk(i, j))),
            pl.BlockSpec((tf, D), lambda i, j: (chunk(i, j), 0)),
            pl.BlockSpec((tm, D), lambda i, j: (jnp.minimum(i, n_head - 1), 0)),
        ] + extra_specs,
        out_specs=pl.BlockSpec((tm, D), lambda i, j: (i, 0)),
        out_shape=jax.ShapeDtypeStruct((T, D), F32),
        scratch_shapes=[pltpu.VMEM((tm, D), BF16)],
        compiler_params=_params(),
        name="ffn_final" if final else "ffn",
    )(x, gain2, wg, wu, wo, head, *extra_args)


def _proj_res_kernel(*refs, n_a):
    a_refs, w_refs = refs[:n_a], refs[n_a:2 * n_a]
    x_ref, o_ref = refs[2 * n_a:]
    acc = x_ref[...]
    for a_ref, w_ref in zip(a_refs, w_refs):
        acc = acc + jnp.dot(a_ref[...], w_ref[...].astype(BF16), preferred_element_type=F32)
    o_ref[...] = acc


def _proj_res(a_list, w, layer, x, *, tm, tn):
    T, N = x.shape
    kg = a_list[0].shape[1]
    assert all(a.shape == (T, kg) for a in a_list) and kg * len(a_list) == w.shape[1]
    assert T % tm == 0 and N % tn == 0
    n_a = len(a_list)
    in_specs = [pl.BlockSpec((tm, kg), lambda i, j: (i, 0)) for _ in a_list]
    in_specs += [pl.BlockSpec((None, kg, tn), lambda i, j, g=g: (layer, g, j)) for g in range(n_a)]
    in_specs.append(pl.BlockSpec((tm, tn), lambda i, j: (i, j)))
    return pl.pallas_call(
        functools.partial(_proj_res_kernel, n_a=n_a),
        grid=(T // tm, N // tn),
        in_specs=in_specs,
        out_specs=pl.BlockSpec((tm, tn), lambda i, j: (i, j)),
        out_shape=jax.ShapeDtypeStruct((T, N), F32),
        compiler_params=_params(),
        name="proj_res",
    )(*a_list, *([w] * n_a), x)


def _cumsum_kernel(tri_ref, x_ref, ccol_ref, crow_ref, *, heads):
    pieces = _split3(x_ref[0] * LOG2E)
    tri = tri_ref[...]
    c = sum(jnp.dot(tri, p, preferred_element_type=F32) for p in pieces)
    ccol_ref[0] = c
    crow_ref[0] = c.T[:heads]


def _fox_cumsum(logf, heads):
    B, S, _ = logf.shape
    tri = jnp.tril(jnp.ones((S, S), BF16))
    return pl.pallas_call(
        functools.partial(_cumsum_kernel, heads=heads),
        grid=(B,),
        in_specs=[pl.BlockSpec((S, S), lambda b: (0, 0)),
                  pl.BlockSpec((1, S, LANES), lambda b: (b, 0, 0))],
        out_specs=[pl.BlockSpec((1, S, LANES), lambda b: (b, 0, 0)),
                   pl.BlockSpec((1, heads, S), lambda b: (b, 0, 0))],
        out_shape=[jax.ShapeDtypeStruct((B, S, LANES), F32),
                   jax.ShapeDtypeStruct((B, heads, S), F32)],
        compiler_params=_params(),
        name="fox_cumsum",
    )(tri, logf)


def _fox_attn_kernel(q_ref, k_ref, v_ref, ccol_ref, crow_ref, o_ref, *, tile, hpb):
    hb = pl.program_id(1)
    qi = pl.program_id(2)
    lane = lax.broadcasted_iota(jnp.int32, (tile, LANES), 1)
    ccol = ccol_ref[0]
    heads = range(hpb)
    q = [q_ref[0, :, e * HEAD_DIM:(e + 1) * HEAD_DIM] for e in heads]
    cq = [jnp.sum(jnp.where(lane == hb * hpb + e, ccol, 0.0), axis=1, keepdims=True)
          for e in heads]

    def step(j, carry, diagonal):
        start = pl.multiple_of(j * tile, tile)
        out = []
        for e in heads:
            m, l, acc = carry[e]
            k = k_ref[0, pl.ds(start, tile), e * HEAD_DIM:(e + 1) * HEAD_DIM]
            v = v_ref[0, pl.ds(start, tile), e * HEAD_DIM:(e + 1) * HEAD_DIM]
            ck = crow_ref[0, e, pl.ds(j, 1), :]
            s = lax.dot_general(q[e], k, _NT, preferred_element_type=F32) - ck
            if diagonal:
                row = lax.broadcasted_iota(jnp.int32, (tile, tile), 0)
                col = lax.broadcasted_iota(jnp.int32, (tile, tile), 1)
                s = jnp.where(col <= row, s, NEG_INF)
            m_new = jnp.maximum(m, jnp.max(s, axis=1, keepdims=True) + cq[e])
            alpha = jnp.exp2(m - m_new)
            p = jnp.exp2(s - (m_new - cq[e]))
            l = alpha * l + jnp.sum(p, axis=1, keepdims=True)
            acc = alpha * acc + jnp.dot(p.astype(BF16), v, preferred_element_type=F32)
            out.append((m_new, l, acc))
        return tuple(out)

    init = tuple((jnp.full((tile, 1), NEG_INF, F32), jnp.zeros((tile, 1), F32),
                  jnp.zeros((tile, HEAD_DIM), F32)) for _ in heads)
    carry = lax.fori_loop(0, qi, lambda j, c: step(j, c, False), init)
    carry = step(qi, carry, True)
    for e in heads:
        _, l, acc = carry[e]
        o_ref[0, :, e * HEAD_DIM:(e + 1) * HEAD_DIM] = (acc / l).astype(o_ref.dtype)


def _fox_attention(qkv, ccol, crow, *, heads, tile, hpb=2):
    B, S, _ = qkv.shape
    assert S % tile == 0 and heads % hpb == 0
    nq = S // tile
    nhb = heads // hpb
    w = hpb * HEAD_DIM
    crow4 = crow.reshape(B, heads, nq, tile)
    return pl.pallas_call(
        functools.partial(_fox_attn_kernel, tile=tile, hpb=hpb),
        grid=(B, nhb, nq),
        in_specs=[
            pl.BlockSpec((1, tile, w), lambda b, h, i: (b, i, h)),
            pl.BlockSpec((1, S, w), lambda b, h, i: (b, 0, nhb + h)),
            pl.BlockSpec((1, S, w), lambda b, h, i: (b, 0, 2 * nhb + h)),
            pl.BlockSpec((1, tile, LANES), lambda b, h, i: (b, i, 0)),
            pl.BlockSpec((1, hpb, nq, tile), lambda b, h, i: (b, h, 0, 0)),
        ],
        out_specs=pl.BlockSpec((1, tile, w), lambda b, h, i: (b, i, h)),
        out_shape=jax.ShapeDtypeStruct((B, S, heads * HEAD_DIM), BF16),
        compiler_params=_params(),
        name="fox_attn",
    )(qkv, qkv, qkv, ccol, crow4)


def _t5_bucket(dist):
    max_exact = NUM_BUCKETS // 2
    d = np.maximum(dist, 1).astype(np.float32)
    large = max_exact + (np.log(d / max_exact) / np.log(MAX_DISTANCE / max_exact)
                         * (NUM_BUCKETS - max_exact)).astype(np.int32)
    large = np.minimum(large, NUM_BUCKETS - 1)
    return np.where(dist < max_exact, dist, large).astype(np.int32)


def _dil_bias_tables(rel_bias, patterns, group_heads):
    period = 4 * DIL_TILE
    c = np.arange(period)
    back = c < 2 * DIL_TILE
    groups = []
    for g, (window, dil) in enumerate(patterns):
        assert window // dil == DIL_TILE
        strips = []
        for t in (0, 1):
            delta = np.where(back, t * DIL_TILE - c, t * DIL_TILE + period - c)
            valid = (delta >= 0) & (delta <= DIL_TILE) & (back | (c > period - DIL_TILE))
            bucket = _t5_bucket(np.clip(delta, 0, DIL_TILE) * dil)
            vec = rel_bias[bucket][:, g * group_heads:(g + 1) * group_heads].astype(F32)
            vec = jnp.where(valid[:, None], vec, NEG_INF).T
            flat = jnp.tile(vec, (1, DIL_TILE))[:, :DIL_TILE * (period - 1)]
            strips.append(flat.reshape(group_heads, DIL_TILE, period - 1)[:, :, :2 * DIL_TILE])
        groups.append(jnp.stack(strips, axis=1))
    return jnp.stack(groups)


def _dil_attn_kernel(*refs, dils, seq):
    n_g = len(dils)
    q_refs, k_refs, v_refs = refs[:n_g], refs[n_g:2 * n_g], refs[2 * n_g:3 * n_g]
    t_ref = refs[3 * n_g]
    o_refs = refs[3 * n_g + 1:4 * n_g + 1]
    o_sc, lse_sc = refs[4 * n_g + 1:]

    def rows(start, size, stride):
        return pl.ds(start, size) if stride == 1 else pl.ds(start, size, stride=stride)

    for g, d in enumerate(dils):
        cls_len = seq // d
        n_keys = min(2 * DIL_TILE, cls_len)
        for r in range(d):
            for u0 in range(0, cls_len, DIL_TILE):
                s0 = min(max(u0 - DIL_TILE, 0), cls_len - n_keys)
                strip = (u0 - s0) // DIL_TILE
                q_rows = rows(r + d * u0, DIL_TILE, d)
                k_rows = rows(r + d * s0, n_keys, d)
                q = q_refs[g][0, q_rows, :].astype(BF16)
                k = k_refs[g][0, k_rows, :].astype(BF16)
                v = v_refs[g][0, k_rows, :].astype(BF16)
                s = lax.dot_general(q, k, _NT, preferred_element_type=F32)
                s = s + t_ref[g, 0, strip][:, :n_keys]
                m = jnp.max(s, axis=1, keepdims=True)
                p = jnp.exp(s - m)
                l = jnp.sum(p, axis=1, keepdims=True)
                o = jnp.dot(p.astype(BF16), v, preferred_element_type=F32) / l
                o_sc[g, q_rows, :] = o
                lse_sc[g, q_rows, :] = jnp.broadcast_to(m + jnp.log(l), (DIL_TILE, LANES))

    lse = [lse_sc[g] for g in range(n_g)]
    mx = functools.reduce(jnp.maximum, lse)
    e = [jnp.exp(x - mx) for x in lse]
    inv = 1.0 / functools.reduce(lambda x, y: x + y, e)
    for g in range(n_g):
        o_refs[g][0] = (o_sc[g] * (e[g] * inv)).astype(o_refs[g].dtype)


def _dil_attention(qkv, tables, *, patterns, group_heads):
    B, S, _ = qkv.shape
    n_g = len(patterns)
    dils = tuple(d for _, d in patterns)
    assert all(S % (d * DIL_TILE) == 0 for d in dils)
    nh = n_g * group_heads

    def head_spec(part, g):
        return pl.BlockSpec((1, S, HEAD_DIM),
                            lambda b, h, part=part, g=g: (b, 0, part * nh + g * group_heads + h))

    in_specs = [head_spec(part, g) for part in range(3) for g in range(n_g)]
    in_specs.append(pl.BlockSpec((n_g, 1, 2, DIL_TILE, 2 * DIL_TILE),
                                 lambda b, h: (0, h, 0, 0, 0)))
    out_spec = pl.BlockSpec((1, S, HEAD_DIM), lambda b, h: (b, 0, h))
    return pl.pallas_call(
        functools.partial(_dil_attn_kernel, dils=dils, seq=S),
        grid=(B, group_heads),
        in_specs=in_specs,
        out_specs=[out_spec] * n_g,
        out_shape=[jax.ShapeDtypeStruct((B, S, group_heads * HEAD_DIM), BF16)] * n_g,
        scratch_shapes=[pltpu.VMEM((n_g, S, HEAD_DIM), F32), pltpu.VMEM((n_g, S, LANES), F32)],
        compiler_params=_params(),
        name="dil_attn",
    )(*([qkv] * (3 * n_g)), tables)


def _cross_kernel(x_ref, g_ref, wq_ref, kv_ref, wo_ref, o_ref, *, heads, scale):
    x = x_ref[0]
    h = _rmsnorm(x, g_ref[...]).astype(BF16)
    q = (jnp.dot(h, wq_ref[...].astype(BF16), preferred_element_type=F32) * scale).astype(BF16)
    kv = kv_ref[0]
    outs = []
    for hd in range(heads):
        qh = q[:, hd * HEAD_DIM:(hd + 1) * HEAD_DIM]
        kh = kv[:, hd * HEAD_DIM:(hd + 1) * HEAD_DIM]
        vh = kv[:, (heads + hd) * HEAD_DIM:(heads + hd + 1) * HEAD_DIM]
        s = lax.dot_general(qh, kh, _NT, preferred_element_type=F32)
        p = jnp.exp(s - jnp.max(s, axis=1, keepdims=True))
        l = jnp.sum(p, axis=1, keepdims=True)
        outs.append((jnp.dot(p.astype(BF16), vh, preferred_element_type=F32) / l).astype(BF16))
    o = jnp.concatenate(outs, axis=1)
    o_ref[0] = x + jnp.dot(o, wo_ref[...].astype(BF16), preferred_element_type=F32)


def _cross_attention(x, gain, wq, kv, wo, layer, *, heads, tm):
    B, S, D = x.shape
    M = kv.shape[1]
    hd = heads * HEAD_DIM
    assert S % tm == 0
    return pl.pallas_call(
        functools.partial(_cross_kernel, heads=heads, scale=HEAD_DIM ** -0.5),
        grid=(B, S // tm),
        in_specs=[
            pl.BlockSpec((1, tm, D), lambda b, i: (b, i, 0)),
            pl.BlockSpec((1, D), lambda b, i: (0, 0)),
            pl.BlockSpec((None, D, hd), lambda b, i: (layer, 0, 0)),
            pl.BlockSpec((1, M, 2 * hd), lambda b, i: (b, 0, 0)),
            pl.BlockSpec((None, hd, D), lambda b, i: (layer, 0, 0)),
        ],
        out_specs=pl.BlockSpec((1, tm, D), lambda b, i: (b, i, 0)),
        out_shape=jax.ShapeDtypeStruct((B, S, D), F32),
        compiler_params=_params(),
        name="cross_attn",
    )(x, gain.reshape(1, D), wq, kv, wo)


def _fox_mixer(x, gain, w_in, w_gate, b_f, w_out, layer, *, heads, attn_tile, tm):
    B, S, D = x.shape
    hd = heads * HEAD_DIM
    w_f = jnp.pad(w_gate, ((0, 0), (0, LANES - heads))).astype(BF16)
    b_pad = jnp.pad(b_f, (0, LANES - heads)).reshape(1, LANES).astype(F32)
    qkv, logf = _rms_proj(x.reshape(B * S, D), gain, w_in, layer, n_out=3 * hd, tm=tm,
                          tn=min(1024, hd), out_dtype=BF16, q_cols=hd,
                          q_scale=HEAD_DIM ** -0.5 * LOG2E, gate=(w_f, b_pad))
    ccol, crow = _fox_cumsum(logf.reshape(B, S, LANES), heads)
    o = _fox_attention(qkv.reshape(B, S, 3 * hd), ccol, crow, heads=heads, tile=attn_tile)
    y = _proj_res([o.reshape(B * S, hd)], w_out, layer, x.reshape(B * S, D), tm=tm,
                  tn=min(1024, D))
    return y.reshape(B, S, D)


def _dilated_mixer(x, gain, w_in, w_out, layer, rel_bias, *, patterns, group_heads, tm):
    B, S, D = x.shape
    n_g = len(patterns)
    hd = n_g * group_heads * HEAD_DIM
    qkv = _rms_proj(x.reshape(B * S, D), gain, w_in, layer, n_out=3 * hd, tm=tm,
                    tn=group_heads * HEAD_DIM, out_dtype=F32, q_cols=hd, q_scale=HEAD_DIM ** -0.5)
    tables = _dil_bias_tables(rel_bias, patterns, group_heads)
    outs = _dil_attention(qkv.reshape(B, S, 3 * hd), tables, patterns=patterns,
                          group_heads=group_heads)
    outs = [o.reshape(B * S, group_heads * HEAD_DIM) for o in outs]
    y = _proj_res(outs, w_out, layer, x.reshape(B * S, D), tm=tm, tn=min(1024, D))
    return y.reshape(B, S, D)


def _forward(x, mem, ffn1_norm, ffn1_w_in, ffn1_w_out, mix_norm, fox_w_in, fox_b_f, fox_w_out,
             dil_w_in, dil_w_out, rel_bias, cross_norm, mem_norm, cross_w_q, cross_w_kv,
             cross_w_out, ffn2_norm, ffn2_w_in, ffn2_w_out, final_norm, *,
             patterns, fox_heads, group_heads, cross_heads, tm, tf, head_tm, head_tf, proj_tm,
             attn_tile, cross_tm):
    B, S, D = x.shape
    M = mem.shape[1]
    depth = ffn1_norm.shape[0]
    n_mixers = 2
    fox_hd = fox_heads * HEAD_DIM
    cross_hd = cross_heads * HEAD_DIM
    fox_w_gate = fox_w_in[:, :, 3 * fox_hd:]
    ffn_tiles = dict(tm=tm, tf=tf, head_tm=head_tm, head_tf=head_tf)
    for i in range(depth):
        x = _ffn(x.reshape(B * S, D), ffn1_norm[i], ffn1_w_in, ffn1_w_out, i,
                 **ffn_tiles).reshape(B, S, D)
        j = i // n_mixers
        if i % n_mixers == 0:
            x = _fox_mixer(x, mix_norm[i], fox_w_in, fox_w_gate[j], fox_b_f[j], fox_w_out, j,
                           heads=fox_heads, attn_tile=attn_tile, tm=proj_tm)
        else:
            x = _dilated_mixer(x, mix_norm[i], dil_w_in, dil_w_out, j, rel_bias,
                               patterns=patterns, group_heads=group_heads, tm=proj_tm)
        kv = _rms_proj(mem.reshape(B * M, D), mem_norm, cross_w_kv, i, n_out=2 * cross_hd,
                       tm=min(tm, B * M), tn=min(512, cross_hd), out_dtype=BF16)
        x = _cross_attention(x, cross_norm[i], cross_w_q, kv.reshape(B, M, 2 * cross_hd),
                             cross_w_out, i, heads=cross_heads, tm=cross_tm)
        x = _ffn(x.reshape(B * S, D), ffn2_norm[i], ffn2_w_in, ffn2_w_out, i, **ffn_tiles,
                 final_gain=final_norm if i == depth - 1 else None).reshape(B, S, D)
    return x


def kernel(x, mem, ffn1_norm, ffn1_w_in, ffn1_w_out, mix_norm, fox_w_in, fox_b_f, fox_w_out, dil_w_in, dil_w_out, rel_bias, cross_norm, mem_norm, cross_w_q, cross_w_kv, cross_w_out, ffn2_norm, ffn2_w_in, ffn2_w_out, final_norm):
    return _forward(x, mem, ffn1_norm, ffn1_w_in, ffn1_w_out, mix_norm, fox_w_in, fox_b_f,
                    fox_w_out, dil_w_in, dil_w_out, rel_bias, cross_norm, mem_norm, cross_w_q,
                    cross_w_kv, cross_w_out, ffn2_norm, ffn2_w_in, ffn2_w_out, final_norm,
                    patterns=DIL_PATTERNS, fox_heads=FOX_HEADS, group_heads=DIL_GROUP_HEADS,
                    cross_heads=CROSS_HEADS, tm=512, tf=512, head_tm=1024, head_tf=256,
                    proj_tm=1024, attn_tile=512, cross_tm=512)
```

```python
import functools

import numpy as np
import jax
import jax.numpy as jnp
from jax import lax
from jax.experimental import pallas as pl
from jax.experimental.pallas import tpu as pltpu

HEAD_DIM = 128
FOX_HEADS = 16
DIL_PATTERNS = ((128, 1), (512, 4), (2048, 16))
DIL_GROUP_HEADS = 6
CROSS_HEADS = 4
NUM_BUCKETS = 32
MAX_DISTANCE = 2048
RMS_EPS = 1e-6
NEG_INF = -1e30

LANES = 128
DIL_TILE = 128
VMEM_LIMIT_BYTES = 56 * 1024 * 1024

BF16 = jnp.bfloat16
F32 = jnp.float32
_NT = (((1,), (1,)), ((), ()))
LOG2E = 1.4426950408889634


def _params():
    return pltpu.CompilerParams(vmem_limit_bytes=VMEM_LIMIT_BYTES)


def _rmsnorm(x, g):
    return x * lax.rsqrt(jnp.mean(x * x, axis=-1, keepdims=True) + RMS_EPS) * g


def _log_sigmoid(z):
    return jnp.minimum(z, 0.0) - jnp.log1p(jnp.exp(-jnp.abs(z)))


def _split3(x):
    hi = x.astype(BF16)
    r = x - hi.astype(F32)
    mid = r.astype(BF16)
    lo = (r - mid.astype(F32)).astype(BF16)
    return hi, mid, lo


def _rms_proj_kernel(x_ref, g_ref, w_ref, *rest, tn, q_cols, q_scale, with_gate):
    if with_gate:
        wf_ref, bf_ref, o_ref, logf_ref, h_ref = rest
    else:
        o_ref, h_ref = rest
    j = pl.program_id(1)

    @pl.when(j == 0)
    def _():
        h = _rmsnorm(x_ref[...], g_ref[...]).astype(BF16)
        h_ref[...] = h
        if with_gate:
            z = jnp.dot(h, wf_ref[...], preferred_element_type=F32) + bf_ref[...]
            logf_ref[...] = _log_sigmoid(z)

    acc = jnp.dot(h_ref[...], w_ref[...], preferred_element_type=F32)
    if q_cols:
        acc = acc * jnp.where(j * tn < q_cols, q_scale, 1.0).astype(F32)
    o_ref[...] = acc.astype(o_ref.dtype)


def _rms_proj(x, gain, w, layer, *, n_out, tm, tn, out_dtype, q_cols=0, q_scale=1.0, gate=None):
    T, D = x.shape
    assert T % tm == 0 and n_out % tn == 0 and q_cols % tn == 0 and n_out <= w.shape[2]
    in_specs = [
        pl.BlockSpec((tm, D), lambda i, j: (i, 0)),
        pl.BlockSpec((1, D), lambda i, j: (0, 0)),
        pl.BlockSpec((None, D, tn), lambda i, j: (layer, 0, j)),
    ]
    args = [x, gain.reshape(1, D), w]
    out_shape = [jax.ShapeDtypeStruct((T, n_out), out_dtype)]
    out_specs = [pl.BlockSpec((tm, tn), lambda i, j: (i, j))]
    if gate is not None:
        in_specs += [pl.BlockSpec((D, LANES), lambda i, j: (0, 0)),
                     pl.BlockSpec((1, LANES), lambda i, j: (0, 0))]
        args += list(gate)
        out_shape.append(jax.ShapeDtypeStruct((T, LANES), F32))
        out_specs.append(pl.BlockSpec((tm, LANES), lambda i, j: (i, 0)))
    res = pl.pallas_call(
        functools.partial(_rms_proj_kernel, tn=tn, q_cols=q_cols, q_scale=q_scale,
                          with_gate=gate is not None),
        grid=(T // tm, n_out // tn),
        in_specs=in_specs,
        out_specs=out_specs,
        out_shape=out_shape,
        scratch_shapes=[pltpu.VMEM((tm, D), BF16)],
        compiler_params=_params(),
        name="rms_proj_gate" if gate is not None else "rms_proj",
    )(*args)
    return res if gate is not None else res[0]


def _ffn_start(x_ref, g_ref, h_ref, o_ref):
    x = x_ref[...]
    h_ref[...] = _rmsnorm(x, g_ref[...]).astype(BF16)
    o_ref[...] = x


def _ffn_chunk(h_ref, o_ref, wg, wu, wo):
    h = h_ref[...]
    gate = jnp.dot(h, wg, preferred_element_type=F32)
    up = jnp.dot(h, wu, preferred_element_type=F32)
    act = (gate * jax.nn.sigmoid(gate) * (0.5 * up)).astype(BF16)
    o_ref[...] += jnp.dot(act, wo, preferred_element_type=F32)


def _ffn_head_kernel(x_ref, g_ref, wg_ref, wu_ref, wo_ref, *rest, final_norm):
    if final_norm:
        fg_ref, o_ref, wg_out, wu_out, wo_out, h_ref = rest
    else:
        o_ref, wg_out, wu_out, wo_out, h_ref = rest
    j = pl.program_id(0)

    @pl.when(j == 0)
    def _():
        _ffn_start(x_ref, g_ref, h_ref, o_ref)

    wg, wu, wo = (w[...].astype(BF16) for w in (wg_ref, wu_ref, wo_ref))
    wg_out[...] = wg
    wu_out[...] = wu
    wo_out[...] = wo
    _ffn_chunk(h_ref, o_ref, wg, wu, wo)

    if final_norm:
        @pl.when(j == pl.num_programs(0) - 1)
        def _():
            o_ref[...] = _rmsnorm(o_ref[...], fg_ref[...])


def _ffn_main_kernel(x_ref, g_ref, wg_ref, wu_ref, wo_ref, head_ref, *rest, final_norm, n_head):
    if final_norm:
        fg_ref, o_ref, h_ref = rest
    else:
        o_ref, h_ref = rest
    i = pl.program_id(0)
    j = pl.program_id(1)
    last = pl.num_programs(1) - 1

    @pl.when(jnp.logical_and(i < n_head, j == last))
    def _():
        o_ref[...] = head_ref[...]

    @pl.when(i >= n_head)
    def _():
        @pl.when(j == 0)
        def _():
            _ffn_start(x_ref, g_ref, h_ref, o_ref)

        _ffn_chunk(h_ref, o_ref, wg_ref[...], wu_ref[...], wo_ref[...])

        if final_norm:
            @pl.when(j == last)
            def _():
                o_ref[...] = _rmsnorm(o_ref[...], fg_ref[...])


def _ffn(x, gain, w_in, w_out, layer, *, tm, tf, head_tm, head_tf, final_gain=None):
    T, D = x.shape
    F = w_out.shape[1]
    assert T % tm == 0 and F % tf == 0 and F % head_tf == 0
    assert head_tm % tm == 0 and head_tm <= T
    final = final_gain is not None
    gain2 = gain.reshape(1, D)
    extra_specs = [pl.BlockSpec((1, D), lambda *_: (0, 0))] if final else []
    extra_args = [final_gain.reshape(1, D)] if final else []

    nfh = F // head_tf
    per = tf // head_tf
    assert tf % head_tf == 0
    once = dict(pipeline_mode=pl.Buffered(1))
    head, wg, wu, wo = pl.pallas_call(
        functools.partial(_ffn_head_kernel, final_norm=final),
        grid=(nfh,),
        in_specs=[
            pl.BlockSpec((head_tm, D), lambda j: (0, 0), **once),
            pl.BlockSpec((1, D), lambda j: (0, 0)),
            pl.BlockSpec((None, D, head_tf), lambda j: (layer, 0, j)),
            pl.BlockSpec((None, D, head_tf), lambda j: (layer, 0, j + nfh)),
            pl.BlockSpec((None, head_tf, D), lambda j: (layer, j, 0)),
        ] + extra_specs,
        out_specs=[
            pl.BlockSpec((head_tm, D), lambda j: (0, 0), **once),
            pl.BlockSpec((None, D, head_tf), lambda j: (j // per, 0, j % per)),
            pl.BlockSpec((None, D, head_tf), lambda j: (j // per, 0, j % per)),
            pl.BlockSpec((head_tf, D), lambda j: (j, 0)),
        ],
        out_shape=[
            jax.ShapeDtypeStruct((head_tm, D), F32),
            jax.ShapeDtypeStruct((F // tf, D, tf), BF16),
            jax.ShapeDtypeStruct((F // tf, D, tf), BF16),
            jax.ShapeDtypeStruct((F, D), BF16),
        ],
        scratch_shapes=[pltpu.VMEM((head_tm, D), BF16)],
        compiler_params=_params(),
        name="ffn_head_final" if final else "ffn_head",
    )(x, gain2, w_in, w_in, w_out, *extra_args)

    n_head = head_tm // tm

    def chunk(i, j):
        return jnp.where(i < n_head, 0, j)

    return pl.pallas_call(
        functools.partial(_ffn_main_kernel, final_norm=final, n_head=n_head),
        grid=(T // tm, F // tf),
        in_specs=[
            pl.BlockSpec((tm, D), lambda i, j: (i, 0)),
            pl.BlockSpec((1, D), lambda i, j: (0, 0)),
            pl.BlockSpec((None, D, tf), lambda i, j: (chunk(i, j), 0, 0)),
            pl.BlockSpec((None, D, tf), lambda i, j: (chunk(i, j), 0, 0)),
            pl.BlockSpec((tf, D), lambda i, j: (chunk(i, j), 0)),
            pl.BlockSpec((tm, D), lambda i, j: (jnp.minimum(i, n_head - 1), 0)),
        ] + extra_specs,
        out_specs=pl.BlockSpec((tm, D), lambda i, j: (i, 0)),
        out_shape=jax.ShapeDtypeStruct((T, D), F32),
        scratch_shapes=[pltpu.VMEM((tm, D), BF16)],
        compiler_params=_params(),
        name="ffn_final" if final else "ffn",
    )(x, gain2, wg, wu, wo, head, *extra_args)


def _cumsum_kernel(tri_ref, x_ref, ccol_ref, crow_ref, *, heads):
    pieces = _split3(x_ref[0] * LOG2E)
    tri = tri_ref[...]
    c = sum(jnp.dot(tri, p, preferred_element_type=F32) for p in pieces)
    ccol_ref[0] = c
    crow_ref[0] = c.T[:heads]


def _fox_cumsum(logf, heads):
    B, S, _ = logf.shape
    tri = jnp.tril(jnp.ones((S, S), BF16))
    return pl.pallas_call(
        functools.partial(_cumsum_kernel, heads=heads),
        grid=(B,),
        in_specs=[pl.BlockSpec((S, S), lambda b: (0, 0)),
                  pl.BlockSpec((1, S, LANES), lambda b: (b, 0, 0))],
        out_specs=[pl.BlockSpec((1, S, LANES), lambda b: (b, 0, 0)),
                   pl.BlockSpec((1, heads, S), lambda b: (b, 0, 0))],
        out_shape=[jax.ShapeDtypeStruct((B, S, LANES), F32),
                   jax.ShapeDtypeStruct((B, heads, S), F32)],
        compiler_params=_params(),
        name="fox_cumsum",
    )(tri, logf)


def _fox_attn_kernel(q_ref, k_ref, v_ref, ccol_ref, crow_ref, o_ref, *, tile, hpb):
    hb = pl.program_id(1)
    qi = pl.program_id(2)
    lane = lax.broadcasted_iota(jnp.int32, (tile, LANES), 1)
    ccol = ccol_ref[0]
    heads = range(hpb)
    q = [q_ref[0, :, e * HEAD_DIM:(e + 1) * HEAD_DIM] for e in heads]
    cq = [jnp.sum(jnp.where(lane == hb * hpb + e, ccol, 0.0), axis=1, keepdims=True)
          for e in heads]

    def step(j, carry, diagonal):
        start = pl.multiple_of(j * tile, tile)
        out = []
        for e in heads:
            m, l, acc = carry[e]
            k = k_ref[0, pl.ds(start, tile), e * HEAD_DIM:(e + 1) * HEAD_DIM]
            v = v_ref[0, pl.ds(start, tile), e * HEAD_DIM:(e + 1) * HEAD_DIM]
            ck = crow_ref[0, e, pl.ds(j, 1), :]
            s = lax.dot_general(q[e], k, _NT, preferred_element_type=F32) - ck
            if diagonal:
                row = lax.broadcasted_iota(jnp.int32, (tile, tile), 0)
                col = lax.broadcasted_iota(jnp.int32, (tile, tile), 1)
                s = jnp.where(col <= row, s, NEG_INF)
            m_new = jnp.maximum(m, jnp.max(s, axis=1, keepdims=True) + cq[e])
            alpha = jnp.exp2(m - m_new)
            p = jnp.exp2(s - (m_new - cq[e]))
            l = alpha * l + jnp.sum(p, axis=1, keepdims=True)
            acc = alpha * acc + jnp.dot(p.astype(BF16), v, preferred_element_type=F32)
            out.append((m_new, l, acc))
        return tuple(out)

    init = tuple((jnp.full((tile, 1), NEG_INF, F32), jnp.zeros((tile, 1), F32),
                  jnp.zeros((tile, HEAD_DIM), F32)) for _ in heads)
    carry = lax.fori_loop(0, qi, lambda j, c: step(j, c, False), init)
    carry = step(qi, carry, True)
    for e in heads:
        _, l, acc = carry[e]
        o_ref[0, :, e * HEAD_DIM:(e + 1) * HEAD_DIM] = (acc / l).astype(o_ref.dtype)


def _fox_attention(qkv, ccol, crow, *, heads, tile, hpb=2):
    B, S, _ = qkv.shape
    assert S % tile == 0 and heads % hpb == 0
    nq = S // tile
    nhb = heads // hpb
    w = hpb * HEAD_DIM
    crow4 = crow.reshape(B, heads, nq, tile)
    return pl.pallas_call(
        functools.partial(_fox_attn_kernel, tile=tile, hpb=hpb),
        grid=(B, nhb, nq),
        in_specs=[
            pl.BlockSpec((1, tile, w), lambda b, h, i: (b, i, h)),
            pl.BlockSpec((1, S, w), lambda b, h, i: (b, 0, nhb + h)),
            pl.BlockSpec((1, S, w), lambda b, h, i: (b, 0, 2 * nhb + h)),
            pl.BlockSpec((1, tile, LANES), lambda b, h, i: (b, i, 0)),
            pl.BlockSpec((1, hpb, nq, tile), lambda b, h, i: (b, h, 0, 0)),
        ],
        out_specs=pl.BlockSpec((1, tile, w), lambda b, h, i: (b, i, h)),
        out_shape=jax.ShapeDtypeStruct((B, S, heads * HEAD_DIM), BF16),
        compiler_params=_params(),
        name="fox_attn",
    )(qkv, qkv, qkv, ccol, crow4)


def _t5_bucket(dist):
    max_exact = NUM_BUCKETS // 2
    d = np.maximum(dist, 1).astype(np.float32)
    large = max_exact + (np.log(d / max_exact) / np.log(MAX_DISTANCE / max_exact)
                         * (NUM_BUCKETS - max_exact)).astype(np.int32)
    large = np.minimum(large, NUM_BUCKETS - 1)
    return np.where(dist < max_exact, dist, large).astype(np.int32)


def _dil_bias_tables(rel_bias, patterns, group_heads):
    period = 4 * DIL_TILE
    c = np.arange(period)
    back = c < 2 * DIL_TILE
    groups = []
    for g, (window, dil) in enumerate(patterns):
        assert window // dil == DIL_TILE
        strips = []
        for t in (0, 1):
            delta = np.where(back, t * DIL_TILE - c, t * DIL_TILE + period - c)
            valid = (delta >= 0) & (delta <= DIL_TILE) & (back | (c > period - DIL_TILE))
            bucket = _t5_bucket(np.clip(delta, 0, DIL_TILE) * dil)
            vec = rel_bias[bucket][:, g * group_heads:(g + 1) * group_heads].astype(F32)
            vec = jnp.where(valid[:, None], vec, NEG_INF).T
            flat = jnp.tile(vec, (1, DIL_TILE))[:, :DIL_TILE * (period - 1)]
            strips.append(flat.reshape(group_heads, DIL_TILE, period - 1)[:, :, :2 * DIL_TILE])
        groups.append(jnp.stack(strips, axis=1))
    return jnp.stack(groups)


def _dil_attn_kernel(*refs, dils, seq):
    n_g = len(dils)
    q_refs, k_refs, v_refs = refs[:n_g], refs[n_g:2 * n_g], refs[2 * n_g:3 * n_g]
    t_ref = refs[3 * n_g]
    o_refs = refs[3 * n_g + 1:4 * n_g + 1]
    o_sc, lse_sc = refs[4 * n_g + 1:]

    def rows(start, size, stride):
        return pl.ds(start, size) if stride == 1 else pl.ds(start, size, stride=stride)

    for g, d in enumerate(dils):
        cls_len = seq // d
        n_keys = min(2 * DIL_TILE, cls_len)
        for r in range(d):
            for u0 in range(0, cls_len, DIL_TILE):
                s0 = min(max(u0 - DIL_TILE, 0), cls_len - n_keys)
                strip = (u0 - s0) // DIL_TILE
                q_rows = rows(r + d * u0, DIL_TILE, d)
                k_rows = rows(r + d * s0, n_keys, d)
                q = q_refs[g][0, q_rows, :].astype(BF16)
                k = k_refs[g][0, k_rows, :].astype(BF16)
                v = v_refs[g][0, k_rows, :].astype(BF16)
                s = lax.dot_general(q, k, _NT, preferred_element_type=F32)
                s = s + t_ref[g, 0, strip][:, :n_keys]
                m = jnp.max(s, axis=1, keepdims=True)
                p = jnp.exp(s - m)
                l = jnp.sum(p, axis=1, keepdims=True)
                o = jnp.dot(p.astype(BF16), v, preferred_element_type=F32) / l
                o_sc[g, q_rows, :] = o
                lse_sc[g, q_rows, :] = jnp.broadcast_to(m + jnp.log(l), (DIL_TILE, LANES))

    lse = [lse_sc[g] for g in range(n_g)]
    mx = functools.reduce(jnp.maximum, lse)
    e = [jnp.exp(x - mx) for x in lse]
    inv = 1.0 / functools.reduce(lambda x, y: x + y, e)
    for g in range(n_g):
        o_refs[g][0] = (o_sc[g] * (e[g] * inv)).astype(o_refs[g].dtype)


def _dil_attention(qkv, tables, *, patterns, group_heads):
    B, S, _ = qkv.shape
    n_g = len(patterns)
    dils = tuple(d for _, d in patterns)
    assert all(S % (d * DIL_TILE) == 0 for d in dils)
    nh = n_g * group_heads

    def head_spec(part, g):
        return pl.BlockSpec((1, S, HEAD_DIM),
                            lambda b, h, part=part, g=g: (b, 0, part * nh + g * group_heads + h))

    in_specs = [head_spec(part, g) for part in range(3) for g in range(n_g)]
    in_specs.append(pl.BlockSpec((n_g, 1, 2, DIL_TILE, 2 * DIL_TILE),
                                 lambda b, h: (0, h, 0, 0, 0)))
    out_spec = pl.BlockSpec((1, S, HEAD_DIM), lambda b, h: (b, 0, h))
    return pl.pallas_call(
        functools.partial(_dil_attn_kernel, dils=dils, seq=S),
        grid=(B, group_heads),
        in_specs=in_specs,
        out_specs=[out_spec] * n_g,
        out_shape=[jax.ShapeDtypeStruct((B, S, group_heads * HEAD_DIM), BF16)] * n_g,
        scratch_shapes=[pltpu.VMEM((n_g, S, HEAD_DIM), F32), pltpu.VMEM((n_g, S, LANES), F32)],
        compiler_params=_params(),
        name="dil_attn",
    )(*([qkv] * (3 * n_g)), tables)


def _mix_cross_kernel(*refs, n_a, heads, scale):
    a_refs, wm_refs = refs[:n_a], refs[n_a:2 * n_a]
    x_ref, g_ref, wq_ref, kv_ref, wo_ref, o_ref = refs[2 * n_a:]
    x = x_ref[0]
    for a_ref, wm_ref in zip(a_refs, wm_refs):
        x = x + jnp.dot(a_ref[0], wm_ref[...], preferred_element_type=F32)
    h = _rmsnorm(x, g_ref[...]).astype(BF16)
    q = (jnp.dot(h, wq_ref[...].astype(BF16), preferred_element_type=F32) * scale).astype(BF16)
    kv = kv_ref[0]
    outs = []
    for hd in range(heads):
        qh = q[:, hd * HEAD_DIM:(hd + 1) * HEAD_DIM]
        kh = kv[:, hd * HEAD_DIM:(hd + 1) * HEAD_DIM]
        vh = kv[:, (heads + hd) * HEAD_DIM:(heads + hd + 1) * HEAD_DIM]
        s = lax.dot_general(qh, kh, _NT, preferred_element_type=F32)
        p = jnp.exp(s - jnp.max(s, axis=1, keepdims=True))
        l = jnp.sum(p, axis=1, keepdims=True)
        outs.append((jnp.dot(p.astype(BF16), vh, preferred_element_type=F32) / l).astype(BF16))
    o = jnp.concatenate(outs, axis=1)
    o_ref[0] = x + jnp.dot(o, wo_ref[...].astype(BF16), preferred_element_type=F32)


def _mix_cross(x, a_list, w_mix, mix_layer, gain, wq, kv, wo, layer, *, heads, tm):
    B, S, D = x.shape
    M = kv.shape[1]
    hd = heads * HEAD_DIM
    kg = a_list[0].shape[2]
    n_a = len(a_list)
    assert S % tm == 0 and kg * n_a == w_mix.shape[1]
    once = dict(pipeline_mode=pl.Buffered(1))
    in_specs = [pl.BlockSpec((1, tm, kg), lambda b, i: (b, i, 0)) for _ in a_list]
    in_specs += [pl.BlockSpec((None, kg, D), lambda b, i, g=g: (mix_layer, g, 0), **once)
                 for g in range(n_a)]
    in_specs += [
        pl.BlockSpec((1, tm, D), lambda b, i: (b, i, 0)),
        pl.BlockSpec((1, D), lambda b, i: (0, 0)),
        pl.BlockSpec((None, D, hd), lambda b, i: (layer, 0, 0), **once),
        pl.BlockSpec((1, M, 2 * hd), lambda b, i: (b, 0, 0)),
        pl.BlockSpec((None, hd, D), lambda b, i: (layer, 0, 0), **once),
    ]
    return pl.pallas_call(
        functools.partial(_mix_cross_kernel, n_a=n_a, heads=heads, scale=HEAD_DIM ** -0.5),
        grid=(B, S // tm),
        in_specs=in_specs,
        out_specs=pl.BlockSpec((1, tm, D), lambda b, i: (b, i, 0)),
        out_shape=jax.ShapeDtypeStruct((B, S, D), F32),
        compiler_params=_params(),
        name="mix_cross",
    )(*a_list, *([w_mix] * n_a), x, gain.reshape(1, D), wq, kv, wo)


def _fox_mixer(x, gain, w_in, w_gate, b_f, layer, *, heads, attn_tile, tm):
    B, S, D = x.shape
    hd = heads * HEAD_DIM
    w_f = jnp.pad(w_gate, ((0, 0), (0, LANES - heads))).astype(BF16)
    b_pad = jnp.pad(b_f, (0, LANES - heads)).reshape(1, LANES).astype(F32)
    qkv, logf = _rms_proj(x.reshape(B * S, D), gain, w_in, layer, n_out=3 * hd, tm=tm,
                          tn=min(1024, hd), out_dtype=BF16, q_cols=hd,
                          q_scale=HEAD_DIM ** -0.5 * LOG2E, gate=(w_f, b_pad))
    ccol, crow = _fox_cumsum(logf.reshape(B, S, LANES), heads)
    return [_fox_attention(qkv.reshape(B, S, 3 * hd), ccol, crow, heads=heads, tile=attn_tile)]


def _dilated_mixer(x, gain, w_in, layer, rel_bias, *, patterns, group_heads, tm):
    B, S, D = x.shape
    n_g = len(patterns)
    hd = n_g * group_heads * HEAD_DIM
    qkv = _rms_proj(x.reshape(B * S, D), gain, w_in, layer, n_out=3 * hd, tm=tm,
                    tn=group_heads * HEAD_DIM, out_dtype=F32, q_cols=hd, q_scale=HEAD_DIM ** -0.5)
    tables = _dil_bias_tables(rel_bias, patterns, group_heads)
    return _dil_attention(qkv.reshape(B, S, 3 * hd), tables, patterns=patterns,
                          group_heads=group_heads)


def _forward(x, mem, ffn1_norm, ffn1_w_in, ffn1_w_out, mix_norm, fox_w_in, fox_b_f, fox_w_out,
             dil_w_in, dil_w_out, rel_bias, cross_norm, mem_norm, cross_w_q, cross_w_kv,
             cross_w_out, ffn2_norm, ffn2_w_in, ffn2_w_out, final_norm, *,
             patterns, fox_heads, group_heads, cross_heads, tm, tf, head_tm, head_tf, proj_tm,
             attn_tile, cross_tm):
    B, S, D = x.shape
    M = mem.shape[1]
    depth = ffn1_norm.shape[0]
    n_mixers = 2
    fox_hd = fox_heads * HEAD_DIM
    cross_hd = cross_heads * HEAD_DIM
    fox_w_gate = fox_w_in[:, :, 3 * fox_hd:]
    fox_w_in, fox_w_out, dil_w_in, dil_w_out, cross_w_kv = (
        w.astype(BF16) for w in (fox_w_in, fox_w_out, dil_w_in, dil_w_out, cross_w_kv))
    ffn_tiles = dict(tm=tm, tf=tf, head_tm=head_tm, head_tf=head_tf)
    for i in range(depth):
        x = _ffn(x.reshape(B * S, D), ffn1_norm[i], ffn1_w_in, ffn1_w_out, i,
                 **ffn_tiles).reshape(B, S, D)
        j = i // n_mixers
        if i % n_mixers == 0:
            mixed = _fox_mixer(x, mix_norm[i], fox_w_in, fox_w_gate[j], fox_b_f[j], j,
                               heads=fox_heads, attn_tile=attn_tile, tm=proj_tm)
            w_mix = fox_w_out
        else:
            mixed = _dilated_mixer(x, mix_norm[i], dil_w_in, j, rel_bias,
                                   patterns=patterns, group_heads=group_heads, tm=proj_tm)
            w_mix = dil_w_out
        kv = _rms_proj(mem.reshape(B * M, D), mem_norm, cross_w_kv, i, n_out=2 * cross_hd,
                       tm=min(tm, B * M), tn=min(512, cross_hd), out_dtype=BF16)
        x = _mix_cross(x, mixed, w_mix, j, cross_norm[i], cross_w_q,
                       kv.reshape(B, M, 2 * cross_hd), cross_w_out, i,
                       heads=cross_heads, tm=cross_tm)
        x = _ffn(x.reshape(B * S, D), ffn2_norm[i], ffn2_w_in, ffn2_w_out, i, **ffn_tiles,
                 final_gain=final_norm if i == depth - 1 else None).reshape(B, S, D)
    return x


def kernel(x, mem, ffn1_norm, ffn1_w_in, ffn1_w_out, mix_norm, fox_w_in, fox_b_f, fox_w_out, dil_w_in, dil_w_out, rel_bias, cross_norm, mem_norm, cross_w_q, cross_w_kv, cross_w_out, ffn2_norm, ffn2_w_in, ffn2_w_out, final_norm):
    return _forward(x, mem, ffn1_norm, ffn1_w_in, ffn1_w_out, mix_norm, fox_w_in, fox_b_f,
                    fox_w_out, dil_w_in, dil_w_out, rel_bias, cross_norm, mem_norm, cross_w_q,
                    cross_w_kv, cross_w_out, ffn2_norm, ffn2_w_in, ffn2_w_out, final_norm,
                    patterns=DIL_PATTERNS, fox_heads=FOX_HEADS, group_heads=DIL_GROUP_HEADS,
                    cross_heads=CROSS_HEADS, tm=512, tf=512, head_tm=1024, head_tf=256,
                    proj_tm=1024, attn_tile=512, cross_tm=512)
```

```python
import functools

import numpy as np
import jax
import jax.numpy as jnp
from jax import lax
from jax.experimental import pallas as pl
from jax.experimental.pallas import tpu as pltpu

HEAD_DIM = 128
FOX_HEADS = 16
DIL_PATTERNS = ((128, 1), (512, 4), (2048, 16))
DIL_GROUP_HEADS = 6
CROSS_HEADS = 4
NUM_BUCKETS = 32
MAX_DISTANCE = 2048
RMS_EPS = 1e-6
NEG_INF = -1e30

LANES = 128
DIL_TILE = 128
VMEM_LIMIT_BYTES = 56 * 1024 * 1024

BF16 = jnp.bfloat16
F32 = jnp.float32
_NT = (((1,), (1,)), ((), ()))
LOG2E = 1.4426950408889634


def _params():
    return pltpu.CompilerParams(vmem_limit_bytes=VMEM_LIMIT_BYTES)


def _rmsnorm(x, g):
    return x * lax.rsqrt(jnp.mean(x * x, axis=-1, keepdims=True) + RMS_EPS) * g


def _log_sigmoid(z):
    return jnp.minimum(z, 0.0) - jnp.log1p(jnp.exp(-jnp.abs(z)))


def _early_next_tile(n_tiles):
    return lambda i, j: (jnp.where(j == 0, i, jnp.minimum(i + 1, n_tiles - 1)), 0)


def _split3(x):
    hi = x.astype(BF16)
    r = x - hi.astype(F32)
    mid = r.astype(BF16)
    lo = (r - mid.astype(F32)).astype(BF16)
    return hi, mid, lo


def _rms_proj_kernel(x_ref, g_ref, w_ref, *rest, tn, q_cols, q_scale, with_gate):
    if with_gate:
        wf_ref, bf_ref, o_ref, logf_ref, h_ref = rest
    else:
        o_ref, h_ref = rest
    j = pl.program_id(1)

    @pl.when(j == 0)
    def _():
        h = _rmsnorm(x_ref[...], g_ref[...]).astype(BF16)
        h_ref[...] = h
        if with_gate:
            z = jnp.dot(h, wf_ref[...], preferred_element_type=F32) + bf_ref[...]
            logf_ref[...] = _log_sigmoid(z)

    acc = jnp.dot(h_ref[...], w_ref[...], preferred_element_type=F32)
    if q_cols:
        acc = acc * jnp.where(j * tn < q_cols, q_scale, 1.0).astype(F32)
    o_ref[...] = acc.astype(o_ref.dtype)


def _rms_proj(x, gain, w, layer, *, n_out, tm, tn, out_dtype, q_cols=0, q_scale=1.0, gate=None):
    T, D = x.shape
    assert T % tm == 0 and n_out % tn == 0 and q_cols % tn == 0 and n_out <= w.shape[2]
    in_specs = [
        pl.BlockSpec((tm, D), _early_next_tile(T // tm)),
        pl.BlockSpec((1, D), lambda i, j: (0, 0)),
        pl.BlockSpec((None, D, tn), lambda i, j: (layer, 0, j)),
    ]
    args = [x, gain.reshape(1, D), w]
    out_shape = [jax.ShapeDtypeStruct((T, n_out), out_dtype)]
    out_specs = [pl.BlockSpec((tm, tn), lambda i, j: (i, j))]
    if gate is not None:
        in_specs += [pl.BlockSpec((D, LANES), lambda i, j: (0, 0)),
                     pl.BlockSpec((1, LANES), lambda i, j: (0, 0))]
        args += list(gate)
        out_shape.append(jax.ShapeDtypeStruct((T, LANES), F32))
        out_specs.append(pl.BlockSpec((tm, LANES), lambda i, j: (i, 0)))
    res = pl.pallas_call(
        functools.partial(_rms_proj_kernel, tn=tn, q_cols=q_cols, q_scale=q_scale,
                          with_gate=gate is not None),
        grid=(T // tm, n_out // tn),
        in_specs=in_specs,
        out_specs=out_specs,
        out_shape=out_shape,
        scratch_shapes=[pltpu.VMEM((tm, D), BF16)],
        compiler_params=_params(),
        name="rms_proj_gate" if gate is not None else "rms_proj",
    )(*args)
    return res if gate is not None else res[0]


def _ffn_start(x_ref, g_ref, h_ref, o_ref):
    x = x_ref[...]
    h_ref[...] = _rmsnorm(x, g_ref[...]).astype(BF16)
    o_ref[...] = x


def _ffn_chunk(h_ref, o_ref, wg, wu, wo):
    h = h_ref[...]
    gate = jnp.dot(h, wg, preferred_element_type=F32)
    up = jnp.dot(h, wu, preferred_element_type=F32)
    act = (gate * jax.nn.sigmoid(gate) * (0.5 * up)).astype(BF16)
    o_ref[...] += jnp.dot(act, wo, preferred_element_type=F32)


def _ffn_head_kernel(x_ref, g_ref, wg_ref, wu_ref, wo_ref, *rest, final_norm):
    if final_norm:
        fg_ref, o_ref, wg_out, wu_out, wo_out, h_ref = rest
    else:
        o_ref, wg_out, wu_out, wo_out, h_ref = rest
    j = pl.program_id(0)

    @pl.when(j == 0)
    def _():
        _ffn_start(x_ref, g_ref, h_ref, o_ref)

    wg, wu, wo = (w[...].astype(BF16) for w in (wg_ref, wu_ref, wo_ref))
    wg_out[...] = wg
    wu_out[...] = wu
    wo_out[...] = wo
    _ffn_chunk(h_ref, o_ref, wg, wu, wo)

    if final_norm:
        @pl.when(j == pl.num_programs(0) - 1)
        def _():
            o_ref[...] = _rmsnorm(o_ref[...], fg_ref[...])


def _ffn_main_kernel(x_ref, g_ref, wg_ref, wu_ref, wo_ref, head_ref, *rest, final_norm, n_head):
    if final_norm:
        fg_ref, o_ref, h_ref = rest
    else:
        o_ref, h_ref = rest
    i = pl.program_id(0)
    j = pl.program_id(1)
    last = pl.num_programs(1) - 1

    @pl.when(jnp.logical_and(i < n_head, j == last))
    def _():
        o_ref[...] = head_ref[...]

    @pl.when(i >= n_head)
    def _():
        @pl.when(j == 0)
        def _():
            _ffn_start(x_ref, g_ref, h_ref, o_ref)

        _ffn_chunk(h_ref, o_ref, wg_ref[...], wu_ref[...], wo_ref[...])

        if final_norm:
            @pl.when(j == last)
            def _():
                o_ref[...] = _rmsnorm(o_ref[...], fg_ref[...])


def _ffn(x, gain, w_in, w_out, layer, *, tm, tf, head_tm, head_tf, final_gain=None):
    T, D = x.shape
    F = w_out.shape[1]
    assert T % tm == 0 and F % tf == 0 and F % head_tf == 0
    assert head_tm % tm == 0 and head_tm <= T
    final = final_gain is not None
    gain2 = gain.reshape(1, D)
    extra_specs = [pl.BlockSpec((1, D), lambda *_: (0, 0))] if final else []
    extra_args = [final_gain.reshape(1, D)] if final else []

    nfh = F // head_tf
    per = tf // head_tf
    assert tf % head_tf == 0
    once = dict(pipeline_mode=pl.Buffered(1))
    head, wg, wu, wo = pl.pallas_call(
        functools.partial(_ffn_head_kernel, final_norm=final),
        grid=(nfh,),
        in_specs=[
            pl.BlockSpec((head_tm, D), lambda j: (0, 0), **once),
            pl.BlockSpec((1, D), lambda j: (0, 0)),
            pl.BlockSpec((None, D, head_tf), lambda j: (layer, 0, j)),
            pl.BlockSpec((None, D, head_tf), lambda j: (layer, 0, j + nfh)),
            pl.BlockSpec((None, head_tf, D), lambda j: (layer, j, 0)),
        ] + extra_specs,
        out_specs=[
            pl.BlockSpec((head_tm, D), lambda j: (0, 0), **once),
            pl.BlockSpec((None, D, head_tf), lambda j: (j // per, 0, j % per)),
            pl.BlockSpec((None, D, head_tf), lambda j: (j // per, 0, j % per)),
            pl.BlockSpec((head_tf, D), lambda j: (j, 0)),
        ],
        out_shape=[
            jax.ShapeDtypeStruct((head_tm, D), F32),
            jax.ShapeDtypeStruct((F // tf, D, tf), BF16),
            jax.ShapeDtypeStruct((F // tf, D, tf), BF16),
            jax.ShapeDtypeStruct((F, D), BF16),
        ],
        scratch_shapes=[pltpu.VMEM((head_tm, D), BF16)],
        compiler_params=_params(),
        name="ffn_head_final" if final else "ffn_head",
    )(x, gain2, w_in, w_in, w_out, *extra_args)

    n_head = head_tm // tm

    def chunk(i, j):
        return jnp.where(i < n_head, 0, j)

    return pl.pallas_call(
        functools.partial(_ffn_main_kernel, final_norm=final, n_head=n_head),
        grid=(T // tm, F // tf),
        in_specs=[
            pl.BlockSpec((tm, D), _early_next_tile(T // tm)),
            pl.BlockSpec((1, D), lambda i, j: (0, 0)),
            pl.BlockSpec((None, D, tf), lambda i, j: (chunk(i, j), 0, 0)),
            pl.BlockSpec((None, D, tf), lambda i, j: (chunk(i, j), 0, 0)),
            pl.BlockSpec((tf, D), lambda i, j: (chunk(i, j), 0)),
            pl.BlockSpec((tm, D), lambda i, j: (jnp.minimum(i, n_head - 1), 0)),
        ] + extra_specs,
        out_specs=pl.BlockSpec((tm, D), lambda i, j: (i, 0)),
        out_shape=jax.ShapeDtypeStruct((T, D), F32),
        scratch_shapes=[pltpu.VMEM((tm, D), BF16)],
        compiler_params=_params(),
        name="ffn_final" if final else "ffn",
    )(x, gain2, wg, wu, wo, head, *extra_args)


def _cumsum_kernel(tri_ref, x_ref, ccol_ref, crow_ref, *, heads):
    tri = tri_ref[...]
    blk = tri.shape[0]
    carry = jnp.zeros((1, LANES), F32)
    for r0 in range(0, x_ref.shape[1], blk):
        pieces = _split3(x_ref[0, r0:r0 + blk, :] * LOG2E)
        c = carry + sum(jnp.dot(tri, p, preferred_element_type=F32) for p in pieces)
        ccol_ref[0, r0:r0 + blk, :] = c
        carry = c[blk - 1:blk, :]
    crow_ref[0] = ccol_ref[0].T[:heads]


def _fox_cumsum(logf, heads, blk=256):
    B, S, _ = logf.shape
    blk = min(blk, S)
    assert S % blk == 0
    tri = jnp.tril(jnp.ones((blk, blk), BF16))
    return pl.pallas_call(
        functools.partial(_cumsum_kernel, heads=heads),
        grid=(B,),
        in_specs=[pl.BlockSpec((blk, blk), lambda b: (0, 0)),
                  pl.BlockSpec((1, S, LANES), lambda b: (b, 0, 0))],
        out_specs=[pl.BlockSpec((1, S, LANES), lambda b: (b, 0, 0)),
                   pl.BlockSpec((1, heads, S), lambda b: (b, 0, 0))],
        out_shape=[jax.ShapeDtypeStruct((B, S, LANES), F32),
                   jax.ShapeDtypeStruct((B, heads, S), F32)],
        compiler_params=_params(),
        name="fox_cumsum",
    )(tri, logf)


def _fox_attn_kernel(q_ref, k_ref, v_ref, ccol_ref, crow_ref, o_ref, *, tile, hpb):
    hb = pl.program_id(1)
    qi = pl.program_id(2)
    lane = lax.broadcasted_iota(jnp.int32, (tile, LANES), 1)
    ccol = ccol_ref[0]
    heads = range(hpb)
    q = [q_ref[0, :, e * HEAD_DIM:(e + 1) * HEAD_DIM] for e in heads]
    cq = [jnp.sum(jnp.where(lane == hb * hpb + e, ccol, 0.0), axis=1, keepdims=True)
          for e in heads]

    def step(j, carry, diagonal):
        start = pl.multiple_of(j * tile, tile)
        out = []
        for e in heads:
            m, l, acc = carry[e]
            k = k_ref[0, pl.ds(start, tile), e * HEAD_DIM:(e + 1) * HEAD_DIM]
            v = v_ref[0, pl.ds(start, tile), e * HEAD_DIM:(e + 1) * HEAD_DIM]
            ck = crow_ref[0, e, pl.ds(j, 1), :]
            s = lax.dot_general(q[e], k, _NT, preferred_element_type=F32) - ck
            if diagonal:
                row = lax.broadcasted_iota(jnp.int32, (tile, tile), 0)
                col = lax.broadcasted_iota(jnp.int32, (tile, tile), 1)
                s = jnp.where(col <= row, s, NEG_INF)
            m_new = jnp.maximum(m, jnp.max(s, axis=1, keepdims=True) + cq[e])
            alpha = jnp.exp2(m - m_new)
            p = jnp.exp2(s - (m_new - cq[e]))
            l = alpha * l + jnp.sum(p, axis=1, keepdims=True)
            acc = alpha * acc + jnp.dot(p.astype(BF16), v, preferred_element_type=F32)
            out.append((m_new, l, acc))
        return tuple(out)

    init = tuple((jnp.full((tile, 1), NEG_INF, F32), jnp.zeros((tile, 1), F32),
                  jnp.zeros((tile, HEAD_DIM), F32)) for _ in heads)
    carry = lax.fori_loop(0, qi, lambda j, c: step(j, c, False), init)
    carry = step(qi, carry, True)
    for e in heads:
        _, l, acc = carry[e]
        o_ref[0, :, e * HEAD_DIM:(e + 1) * HEAD_DIM] = (acc / l).astype(o_ref.dtype)


def _fox_attention(qkv, ccol, crow, *, heads, tile, hpb=2):
    B, S, _ = qkv.shape
    assert S % tile == 0 and heads % hpb == 0
    nq = S // tile
    nhb = heads // hpb
    w = hpb * HEAD_DIM
    crow4 = crow.reshape(B, heads, nq, tile)
    return pl.pallas_call(
        functools.partial(_fox_attn_kernel, tile=tile, hpb=hpb),
        grid=(B, nhb, nq),
        in_specs=[
            pl.BlockSpec((1, tile, w), lambda b, h, i: (b, i, h)),
            pl.BlockSpec((1, S, w), lambda b, h, i: (b, 0, nhb + h)),
            pl.BlockSpec((1, S, w), lambda b, h, i: (b, 0, 2 * nhb + h)),
            pl.BlockSpec((1, tile, LANES), lambda b, h, i: (b, i, 0)),
            pl.BlockSpec((1, hpb, nq, tile), lambda b, h, i: (b, h, 0, 0)),
        ],
        out_specs=pl.BlockSpec((1, tile, w), lambda b, h, i: (b, i, h)),
        out_shape=jax.ShapeDtypeStruct((B, S, heads * HEAD_DIM), BF16),
        compiler_params=_params(),
        name="fox_attn",
    )(qkv, qkv, qkv, ccol, crow4)


def _t5_bucket(dist):
    max_exact = NUM_BUCKETS // 2
    d = np.maximum(dist, 1).astype(np.float32)
    large = max_exact + (np.log(d / max_exact) / np.log(MAX_DISTANCE / max_exact)
                         * (NUM_BUCKETS - max_exact)).astype(np.int32)
    large = np.minimum(large, NUM_BUCKETS - 1)
    return np.where(dist < max_exact, dist, large).astype(np.int32)


def _dil_bias_tables(rel_bias, patterns, group_heads):
    period = 4 * DIL_TILE
    c = np.arange(period)
    back = c < 2 * DIL_TILE
    groups = []
    for g, (window, dil) in enumerate(patterns):
        assert window // dil == DIL_TILE
        strips = []
        for t in (0, 1):
            delta = np.where(back, t * DIL_TILE - c, t * DIL_TILE + period - c)
            valid = (delta >= 0) & (delta <= DIL_TILE) & (back | (c > period - DIL_TILE))
            bucket = _t5_bucket(np.clip(delta, 0, DIL_TILE) * dil)
            vec = rel_bias[bucket][:, g * group_heads:(g + 1) * group_heads].astype(F32)
            vec = jnp.where(valid[:, None], vec, NEG_INF).T
            flat = jnp.tile(vec, (1, DIL_TILE))[:, :DIL_TILE * (period - 1)]
            strips.append(flat.reshape(group_heads, DIL_TILE, period - 1)[:, :, :2 * DIL_TILE])
        groups.append(jnp.stack(strips, axis=1))
    return jnp.stack(groups)


def _dil_attn_kernel(*refs, dils, seq):
    n_g = len(dils)
    q_refs, k_refs, v_refs = refs[:n_g], refs[n_g:2 * n_g], refs[2 * n_g:3 * n_g]
    t_ref = refs[3 * n_g]
    o_refs = refs[3 * n_g + 1:4 * n_g + 1]
    o_sc, lse_sc = refs[4 * n_g + 1:]

    def rows(start, size, stride):
        return pl.ds(start, size) if stride == 1 else pl.ds(start, size, stride=stride)

    for g, d in enumerate(dils):
        cls_len = seq // d
        n_keys = min(2 * DIL_TILE, cls_len)
        for r in range(d):
            for u0 in range(0, cls_len, DIL_TILE):
                s0 = min(max(u0 - DIL_TILE, 0), cls_len - n_keys)
                strip = (u0 - s0) // DIL_TILE
                q_rows = rows(r + d * u0, DIL_TILE, d)
                k_rows = rows(r + d * s0, n_keys, d)
                q = q_refs[g][0, q_rows, :].astype(BF16)
                k = k_refs[g][0, k_rows, :].astype(BF16)
                v = v_refs[g][0, k_rows, :].astype(BF16)
                s = lax.dot_general(q, k, _NT, preferred_element_type=F32)
                s = s + t_ref[g, 0, strip][:, :n_keys]
                m = jnp.max(s, axis=1, keepdims=True)
                p = jnp.exp(s - m)
                l = jnp.sum(p, axis=1, keepdims=True)
                o = jnp.dot(p.astype(BF16), v, preferred_element_type=F32) / l
                o_sc[g, q_rows, :] = o
                lse_sc[g, q_rows, :] = jnp.broadcast_to(m + jnp.log(l), (DIL_TILE, LANES))

    lse = [lse_sc[g] for g in range(n_g)]
    mx = functools.reduce(jnp.maximum, lse)
    e = [jnp.exp(x - mx) for x in lse]
    inv = 1.0 / functools.reduce(lambda x, y: x + y, e)
    for g in range(n_g):
        o_refs[g][0] = (o_sc[g] * (e[g] * inv)).astype(o_refs[g].dtype)


def _dil_attention(qkv, tables, *, patterns, group_heads):
    B, S, _ = qkv.shape
    n_g = len(patterns)
    dils = tuple(d for _, d in patterns)
    assert all(S % (d * DIL_TILE) == 0 for d in dils)
    nh = n_g * group_heads

    def head_spec(part, g):
        return pl.BlockSpec((1, S, HEAD_DIM),
                            lambda b, h, part=part, g=g: (b, 0, part * nh + g * group_heads + h))

    in_specs = [head_spec(part, g) for part in range(3) for g in range(n_g)]
    in_specs.append(pl.BlockSpec((n_g, 1, 2, DIL_TILE, 2 * DIL_TILE),
                                 lambda b, h: (0, h, 0, 0, 0)))
    out_spec = pl.BlockSpec((1, S, HEAD_DIM), lambda b, h: (b, 0, h))
    return pl.pallas_call(
        functools.partial(_dil_attn_kernel, dils=dils, seq=S),
        grid=(B, group_heads),
        in_specs=in_specs,
        out_specs=[out_spec] * n_g,
        out_shape=[jax.ShapeDtypeStruct((B, S, group_heads * HEAD_DIM), BF16)] * n_g,
        scratch_shapes=[pltpu.VMEM((n_g, S, HEAD_DIM), F32), pltpu.VMEM((n_g, S, LANES), F32)],
        compiler_params=_params(),
        name="dil_attn",
    )(*([qkv] * (3 * n_g)), tables)


def _mix_cross_kernel(*refs, n_a, heads, scale):
    a_refs, wm_refs = refs[:n_a], refs[n_a:2 * n_a]
    x_ref, g_ref, wq_ref, kv_ref, wo_ref, o_ref = refs[2 * n_a:]
    x = x_ref[0]
    for a_ref, wm_ref in zip(a_refs, wm_refs):
        x = x + jnp.dot(a_ref[0], wm_ref[...], preferred_element_type=F32)
    h = _rmsnorm(x, g_ref[...]).astype(BF16)
    q = (jnp.dot(h, wq_ref[...].astype(BF16), preferred_element_type=F32) * scale).astype(BF16)
    kv = kv_ref[0]
    outs = []
    for hd in range(heads):
        qh = q[:, hd * HEAD_DIM:(hd + 1) * HEAD_DIM]
        kh = kv[:, hd * HEAD_DIM:(hd + 1) * HEAD_DIM]
        vh = kv[:, (heads + hd) * HEAD_DIM:(heads + hd + 1) * HEAD_DIM]
        s = lax.dot_general(qh, kh, _NT, preferred_element_type=F32)
        p = jnp.exp(s - jnp.max(s, axis=1, keepdims=True))
        l = jnp.sum(p, axis=1, keepdims=True)
        outs.append((jnp.dot(p.astype(BF16), vh, preferred_element_type=F32) / l).astype(BF16))
    o = jnp.concatenate(outs, axis=1)
    o_ref[0] = x + jnp.dot(o, wo_ref[...].astype(BF16), preferred_element_type=F32)


def _mix_cross(x, a_list, w_mix, mix_layer, gain, wq, kv, wo, layer, *, heads, tm):
    B, S, D = x.shape
    M = kv.shape[1]
    hd = heads * HEAD_DIM
    kg = a_list[0].shape[2]
    n_a = len(a_list)
    assert S % tm == 0 and kg * n_a == w_mix.shape[1]
    once = dict(pipeline_mode=pl.Buffered(1))
    in_specs = [pl.BlockSpec((1, tm, kg), lambda b, i: (b, i, 0)) for _ in a_list]
    in_specs += [pl.BlockSpec((None, kg, D), lambda b, i, g=g: (mix_layer, g, 0), **once)
                 for g in range(n_a)]
    in_specs += [
        pl.BlockSpec((1, tm, D), lambda b, i: (b, i, 0)),
        pl.BlockSpec((1, D), lambda b, i: (0, 0)),
        pl.BlockSpec((None, D, hd), lambda b, i: (layer, 0, 0), **once),
        pl.BlockSpec((1, M, 2 * hd), lambda b, i: (b, 0, 0)),
        pl.BlockSpec((None, hd, D), lambda b, i: (layer, 0, 0), **once),
    ]
    return pl.pallas_call(
        functools.partial(_mix_cross_kernel, n_a=n_a, heads=heads, scale=HEAD_DIM ** -0.5),
        grid=(B, S // tm),
        in_specs=in_specs,
        out_specs=pl.BlockSpec((1, tm, D), lambda b, i: (b, i, 0)),
        out_shape=jax.ShapeDtypeStruct((B, S, D), F32),
        compiler_params=_params(),
        name="mix_cross",
    )(*a_list, *([w_mix] * n_a), x, gain.reshape(1, D), wq, kv, wo)


def _fox_mixer(x, gain, w_in, w_gate, b_f, layer, *, heads, attn_tile, tm):
    B, S, D = x.shape
    hd = heads * HEAD_DIM
    w_f = jnp.pad(w_gate, ((0, 0), (0, LANES - heads))).astype(BF16)
    b_pad = jnp.pad(b_f, (0, LANES - heads)).reshape(1, LANES).astype(F32)
    qkv, logf = _rms_proj(x.reshape(B * S, D), gain, w_in, layer, n_out=3 * hd, tm=tm,
                          tn=min(1024, hd), out_dtype=BF16, q_cols=hd,
                          q_scale=HEAD_DIM ** -0.5 * LOG2E, gate=(w_f, b_pad))
    ccol, crow = _fox_cumsum(logf.reshape(B, S, LANES), heads)
    return [_fox_attention(qkv.reshape(B, S, 3 * hd), ccol, crow, heads=heads, tile=attn_tile)]


def _dilated_mixer(x, gain, w_in, layer, rel_bias, *, patterns, group_heads, tm):
    B, S, D = x.shape
    n_g = len(patterns)
    hd = n_g * group_heads * HEAD_DIM
    qkv = _rms_proj(x.reshape(B * S, D), gain, w_in, layer, n_out=3 * hd, tm=tm,
                    tn=group_heads * HEAD_DIM, out_dtype=F32, q_cols=hd, q_scale=HEAD_DIM ** -0.5)
    tables = _dil_bias_tables(rel_bias, patterns, group_heads)
    return _dil_attention(qkv.reshape(B, S, 3 * hd), tables, patterns=patterns,
                          group_heads=group_heads)


def _forward(x, mem, ffn1_norm, ffn1_w_in, ffn1_w_out, mix_norm, fox_w_in, fox_b_f, fox_w_out,
             dil_w_in, dil_w_out, rel_bias, cross_norm, mem_norm, cross_w_q, cross_w_kv,
             cross_w_out, ffn2_norm, ffn2_w_in, ffn2_w_out, final_norm, *,
             patterns, fox_heads, group_heads, cross_heads, tm, tf, head_tm, head_tf, proj_tm,
             attn_tile, cross_tm):
    B, S, D = x.shape
    M = mem.shape[1]
    depth = ffn1_norm.shape[0]
    n_mixers = 2
    fox_hd = fox_heads * HEAD_DIM
    cross_hd = cross_heads * HEAD_DIM
    fox_w_gate = fox_w_in[:, :, 3 * fox_hd:]
    fox_w_in, fox_w_out, dil_w_in, dil_w_out, cross_w_kv = (
        w.astype(BF16) for w in (fox_w_in, fox_w_out, dil_w_in, dil_w_out, cross_w_kv))
    ffn_tiles = dict(tm=tm, tf=tf, head_tm=head_tm, head_tf=head_tf)
    for i in range(depth):
        x = _ffn(x.reshape(B * S, D), ffn1_norm[i], ffn1_w_in, ffn1_w_out, i,
                 **ffn_tiles).reshape(B, S, D)
        j = i // n_mixers
        if i % n_mixers == 0:
            mixed = _fox_mixer(x, mix_norm[i], fox_w_in, fox_w_gate[j], fox_b_f[j], j,
                               heads=fox_heads, attn_tile=attn_tile, tm=proj_tm)
            w_mix = fox_w_out
        else:
            mixed = _dilated_mixer(x, mix_norm[i], dil_w_in, j, rel_bias,
                                   patterns=patterns, group_heads=group_heads, tm=proj_tm)
            w_mix = dil_w_out
        kv = _rms_proj(mem.reshape(B * M, D), mem_norm, cross_w_kv, i, n_out=2 * cross_hd,
                       tm=min(tm, B * M), tn=min(512, cross_hd), out_dtype=BF16)
        x = _mix_cross(x, mixed, w_mix, j, cross_norm[i], cross_w_q,
                       kv.reshape(B, M, 2 * cross_hd), cross_w_out, i,
                       heads=cross_heads, tm=cross_tm)
        x = _ffn(x.reshape(B * S, D), ffn2_norm[i], ffn2_w_in, ffn2_w_out, i, **ffn_tiles,
                 final_gain=final_norm if i == depth - 1 else None).reshape(B, S, D)
    return x


def kernel(x, mem, ffn1_norm, ffn1_w_in, ffn1_w_out, mix_norm, fox_w_in, fox_b_f, fox_w_out, dil_w_in, dil_w_out, rel_bias, cross_norm, mem_norm, cross_w_q, cross_w_kv, cross_w_out, ffn2_norm, ffn2_w_in, ffn2_w_out, final_norm):
    return _forward(x, mem, ffn1_norm, ffn1_w_in, ffn1_w_out, mix_norm, fox_w_in, fox_b_f,
                    fox_w_out, dil_w_in, dil_w_out, rel_bias, cross_norm, mem_norm, cross_w_q,
                    cross_w_kv, cross_w_out, ffn2_norm, ffn2_w_in, ffn2_w_out, final_norm,
                    patterns=DIL_PATTERNS, fox_heads=FOX_HEADS, group_heads=DIL_GROUP_HEADS,
                    cross_heads=CROSS_HEADS, tm=512, tf=512, head_tm=1024, head_tf=256,
                    proj_tm=1024, attn_tile=512, cross_tm=512)
```

```python
import functools

import numpy as np
import jax
import jax.numpy as jnp
from jax import lax
from jax.experimental import pallas as pl
from jax.experimental.pallas import tpu as pltpu

HEAD_DIM = 128
FOX_HEADS = 16
DIL_PATTERNS = ((128, 1), (512, 4), (2048, 16))
DIL_GROUP_HEADS = 6
CROSS_HEADS = 4
NUM_BUCKETS = 32
MAX_DISTANCE = 2048
RMS_EPS = 1e-6
NEG_INF = -1e30

LANES = 128
DIL_TILE = 128
VMEM_LIMIT_BYTES = 56 * 1024 * 1024

BF16 = jnp.bfloat16
F32 = jnp.float32
_NT = (((1,), (1,)), ((), ()))
LOG2E = 1.4426950408889634


def _params():
    return pltpu.CompilerParams(vmem_limit_bytes=VMEM_LIMIT_BYTES)


def _rmsnorm(x, g):
    return x * lax.rsqrt(jnp.mean(x * x, axis=-1, keepdims=True) + RMS_EPS) * g


def _log_sigmoid(z):
    return jnp.minimum(z, 0.0) - jnp.log1p(jnp.exp(-jnp.abs(z)))


def _early_next_tile(n_tiles, switch=1):
    return lambda i, j: (jnp.where(j < switch, i, jnp.minimum(i + 1, n_tiles - 1)), 0)


def _split3(x):
    hi = x.astype(BF16)
    r = x - hi.astype(F32)
    mid = r.astype(BF16)
    lo = (r - mid.astype(F32)).astype(BF16)
    return hi, mid, lo


def _rms_proj_kernel(x_ref, g_ref, w_ref, *rest, tn, q_cols, q_scale, with_gate):
    if with_gate:
        wf_ref, bf_ref, o_ref, logf_ref, h_ref = rest
    else:
        o_ref, h_ref = rest
    j = pl.program_id(1)

    @pl.when(j == 0)
    def _():
        h = _rmsnorm(x_ref[...], g_ref[...]).astype(BF16)
        h_ref[...] = h
        if with_gate:
            z = jnp.dot(h, wf_ref[...], preferred_element_type=F32) + bf_ref[...]
            logf_ref[...] = _log_sigmoid(z)

    acc = jnp.dot(h_ref[...], w_ref[...], preferred_element_type=F32)
    if q_cols:
        acc = acc * jnp.where(j * tn < q_cols, q_scale, 1.0).astype(F32)
    o_ref[...] = acc.astype(o_ref.dtype)


def _rms_proj(x, gain, w, layer, *, n_out, tm, tn, out_dtype, q_cols=0, q_scale=1.0, gate=None):
    T, D = x.shape
    assert T % tm == 0 and n_out % tn == 0 and q_cols % tn == 0 and n_out <= w.shape[2]
    in_specs = [
        pl.BlockSpec((tm, D), _early_next_tile(T // tm)),
        pl.BlockSpec((1, D), lambda i, j: (0, 0)),
        pl.BlockSpec((None, D, tn), lambda i, j: (layer, 0, j)),
    ]
    args = [x, gain.reshape(1, D), w]
    out_shape = [jax.ShapeDtypeStruct((T, n_out), out_dtype)]
    out_specs = [pl.BlockSpec((tm, tn), lambda i, j: (i, j))]
    if gate is not None:
        in_specs += [pl.BlockSpec((D, LANES), lambda i, j: (0, 0)),
                     pl.BlockSpec((1, LANES), lambda i, j: (0, 0))]
        args += list(gate)
        out_shape.append(jax.ShapeDtypeStruct((T, LANES), F32))
        out_specs.append(pl.BlockSpec((tm, LANES), lambda i, j: (i, 0)))
    res = pl.pallas_call(
        functools.partial(_rms_proj_kernel, tn=tn, q_cols=q_cols, q_scale=q_scale,
                          with_gate=gate is not None),
        grid=(T // tm, n_out // tn),
        in_specs=in_specs,
        out_specs=out_specs,
        out_shape=out_shape,
        scratch_shapes=[pltpu.VMEM((tm, D), BF16)],
        compiler_params=_params(),
        name="rms_proj_gate" if gate is not None else "rms_proj",
    )(*args)
    return res if gate is not None else res[0]


def _ffn_start(x_ref, g_ref, h_ref, o_ref):
    x = x_ref[...]
    h_ref[...] = _rmsnorm(x, g_ref[...]).astype(BF16)
    o_ref[...] = x


def _ffn_chunk(h_ref, o_ref, wg, wu, wo):
    h = h_ref[...]
    gate = jnp.dot(h, wg, preferred_element_type=F32)
    up = jnp.dot(h, wu, preferred_element_type=F32)
    act = (gate * jax.nn.sigmoid(gate) * (0.5 * up)).astype(BF16)
    o_ref[...] += jnp.dot(act, wo, preferred_element_type=F32)


def _ffn_head_kernel(x_ref, g_ref, wg_ref, wu_ref, wo_ref, *rest, final_norm):
    if final_norm:
        fg_ref, o_ref, wg_out, wu_out, wo_out, h_ref = rest
    else:
        o_ref, wg_out, wu_out, wo_out, h_ref = rest
    j = pl.program_id(0)

    @pl.when(j == 0)
    def _():
        _ffn_start(x_ref, g_ref, h_ref, o_ref)

    wg, wu, wo = (w[...].astype(BF16) for w in (wg_ref, wu_ref, wo_ref))
    wg_out[...] = wg
    wu_out[...] = wu
    wo_out[...] = wo
    _ffn_chunk(h_ref, o_ref, wg, wu, wo)

    if final_norm:
        @pl.when(j == pl.num_programs(0) - 1)
        def _():
            o_ref[...] = _rmsnorm(o_ref[...], fg_ref[...])


def _ffn_main_kernel(x_ref, g_ref, wg_ref, wu_ref, wo_ref, head_ref, *rest, final_norm, n_head):
    if final_norm:
        fg_ref, o_ref, h_ref = rest
    else:
        o_ref, h_ref = rest
    i = pl.program_id(0)
    j = pl.program_id(1)
    last = pl.num_programs(1) - 1

    @pl.when(jnp.logical_and(i < n_head, j == last))
    def _():
        o_ref[...] = head_ref[...]

    @pl.when(i >= n_head)
    def _():
        @pl.when(j == 0)
        def _():
            _ffn_start(x_ref, g_ref, h_ref, o_ref)

        _ffn_chunk(h_ref, o_ref, wg_ref[...], wu_ref[...], wo_ref[...])

        if final_norm:
            @pl.when(j == last)
            def _():
                o_ref[...] = _rmsnorm(o_ref[...], fg_ref[...])


def _ffn(x, gain, w_in, w_out, layer, *, tm, tf, head_tm, head_tf, final_gain=None):
    T, D = x.shape
    F = w_out.shape[1]
    assert T % tm == 0 and F % tf == 0 and F % head_tf == 0
    assert head_tm % tm == 0 and head_tm <= T
    final = final_gain is not None
    gain2 = gain.reshape(1, D)
    extra_specs = [pl.BlockSpec((1, D), lambda *_: (0, 0))] if final else []
    extra_args = [final_gain.reshape(1, D)] if final else []

    nfh = F // head_tf
    per = tf // head_tf
    assert tf % head_tf == 0
    once = dict(pipeline_mode=pl.Buffered(1))
    head, wg, wu, wo = pl.pallas_call(
        functools.partial(_ffn_head_kernel, final_norm=final),
        grid=(nfh,),
        in_specs=[
            pl.BlockSpec((head_tm, D), lambda j: (0, 0), **once),
            pl.BlockSpec((1, D), lambda j: (0, 0)),
            pl.BlockSpec((None, D, head_tf), lambda j: (layer, 0, j)),
            pl.BlockSpec((None, D, head_tf), lambda j: (layer, 0, j + nfh)),
            pl.BlockSpec((None, head_tf, D), lambda j: (layer, j, 0)),
        ] + extra_specs,
        out_specs=[
            pl.BlockSpec((head_tm, D), lambda j: (0, 0), **once),
            pl.BlockSpec((None, D, head_tf), lambda j: (j // per, 0, j % per)),
            pl.BlockSpec((None, D, head_tf), lambda j: (j // per, 0, j % per)),
            pl.BlockSpec((head_tf, D), lambda j: (j, 0)),
        ],
        out_shape=[
            jax.ShapeDtypeStruct((head_tm, D), F32),
            jax.ShapeDtypeStruct((F // tf, D, tf), BF16),
            jax.ShapeDtypeStruct((F // tf, D, tf), BF16),
            jax.ShapeDtypeStruct((F, D), BF16),
        ],
        scratch_shapes=[pltpu.VMEM((head_tm, D), BF16)],
        compiler_params=_params(),
        name="ffn_head_final" if final else "ffn_head",
    )(x, gain2, w_in, w_in, w_out, *extra_args)

    n_head = head_tm // tm

    def chunk(i, j):
        return jnp.where(i < n_head, 0, j)

    return pl.pallas_call(
        functools.partial(_ffn_main_kernel, final_norm=final, n_head=n_head),
        grid=(T // tm, F // tf),
        in_specs=[
            pl.BlockSpec((tm, D), _early_next_tile(T // tm, switch=(F // tf) // 2)),
            pl.BlockSpec((1, D), lambda i, j: (0, 0)),
            pl.BlockSpec((None, D, tf), lambda i, j: (chunk(i, j), 0, 0)),
            pl.BlockSpec((None, D, tf), lambda i, j: (chunk(i, j), 0, 0)),
            pl.BlockSpec((tf, D), lambda i, j: (chunk(i, j), 0)),
            pl.BlockSpec((tm, D), lambda i, j: (jnp.minimum(i, n_head - 1), 0)),
        ] + extra_specs,
        out_specs=pl.BlockSpec((tm, D), lambda i, j: (i, 0)),
        out_shape=jax.ShapeDtypeStruct((T, D), F32),
        scratch_shapes=[pltpu.VMEM((tm, D), BF16)],
        compiler_params=_params(),
        name="ffn_final" if final else "ffn",
    )(x, gain2, wg, wu, wo, head, *extra_args)


def _cumsum_kernel(tri_ref, x_ref, ccol_ref, crow_ref, *, heads):
    tri = tri_ref[...]
    blk = tri.shape[0]
    carry = jnp.zeros((1, LANES), F32)
    for r0 in range(0, x_ref.shape[1], blk):
        pieces = _split3(x_ref[0, r0:r0 + blk, :] * LOG2E)
        c = carry + sum(jnp.dot(tri, p, preferred_element_type=F32) for p in pieces)
        ccol_ref[0, r0:r0 + blk, :] = c
        carry = c[blk - 1:blk, :]
    crow_ref[0] = ccol_ref[0].T[:heads]


def _fox_cumsum(logf, heads, blk=256):
    B, S, _ = logf.shape
    blk = min(blk, S)
    assert S % blk == 0
    tri = jnp.tril(jnp.ones((blk, blk), BF16))
    return pl.pallas_call(
        functools.partial(_cumsum_kernel, heads=heads),
        grid=(B,),
        in_specs=[pl.BlockSpec((blk, blk), lambda b: (0, 0)),
                  pl.BlockSpec((1, S, LANES), lambda b: (b, 0, 0))],
        out_specs=[pl.BlockSpec((1, S, LANES), lambda b: (b, 0, 0)),
                   pl.BlockSpec((1, heads, S), lambda b: (b, 0, 0))],
        out_shape=[jax.ShapeDtypeStruct((B, S, LANES), F32),
                   jax.ShapeDtypeStruct((B, heads, S), F32)],
        compiler_params=_params(),
        name="fox_cumsum",
    )(tri, logf)


def _fox_attn_kernel(q_ref, k_ref, v_ref, ccol_ref, crow_ref, o_ref, *, tile, hpb):
    hb = pl.program_id(1)
    qi = pl.program_id(2)
    lane = lax.broadcasted_iota(jnp.int32, (tile, LANES), 1)
    ccol = ccol_ref[0]
    heads = range(hpb)
    q = [q_ref[0, :, e * HEAD_DIM:(e + 1) * HEAD_DIM] for e in heads]
    cq = [jnp.sum(jnp.where(lane == hb * hpb + e, ccol, 0.0), axis=1, keepdims=True)
          for e in heads]

    def step(j, carry, diagonal):
        start = pl.multiple_of(j * tile, tile)
        out = []
        for e in heads:
            m, l, acc = carry[e]
            k = k_ref[0, pl.ds(start, tile), e * HEAD_DIM:(e + 1) * HEAD_DIM]
            v = v_ref[0, pl.ds(start, tile), e * HEAD_DIM:(e + 1) * HEAD_DIM]
            ck = crow_ref[0, e, pl.ds(j, 1), :]
            s = lax.dot_general(q[e], k, _NT, preferred_element_type=F32) - ck
            if diagonal:
                row = lax.broadcasted_iota(jnp.int32, (tile, tile), 0)
                col = lax.broadcasted_iota(jnp.int32, (tile, tile), 1)
                s = jnp.where(col <= row, s, NEG_INF)
            m_new = jnp.maximum(m, jnp.max(s, axis=1, keepdims=True) + cq[e])
            alpha = jnp.exp2(m - m_new)
            p = jnp.exp2(s - (m_new - cq[e]))
            l = alpha * l + jnp.sum(p, axis=1, keepdims=True)
            acc = alpha * acc + jnp.dot(p.astype(BF16), v, preferred_element_type=F32)
            out.append((m_new, l, acc))
        return tuple(out)

    init = tuple((jnp.full((tile, 1), NEG_INF, F32), jnp.zeros((tile, 1), F32),
                  jnp.zeros((tile, HEAD_DIM), F32)) for _ in heads)
    carry = lax.fori_loop(0, qi, lambda j, c: step(j, c, False), init)
    carry = step(qi, carry, True)
    for e in heads:
        _, l, acc = carry[e]
        o_ref[0, :, e * HEAD_DIM:(e + 1) * HEAD_DIM] = (acc / l).astype(o_ref.dtype)


def _fox_attention(qkv, ccol, crow, *, heads, tile, hpb=4):
    B, S, _ = qkv.shape
    hpb = min(hpb, heads)
    assert S % tile == 0 and heads % hpb == 0
    nq = S // tile
    nhb = heads // hpb
    w = hpb * HEAD_DIM
    crow4 = crow.reshape(B, heads, nq, tile)
    return pl.pallas_call(
        functools.partial(_fox_attn_kernel, tile=tile, hpb=hpb),
        grid=(B, nhb, nq),
        in_specs=[
            pl.BlockSpec((1, tile, w), lambda b, h, i: (b, i, h)),
            pl.BlockSpec((1, S, w), lambda b, h, i: (b, 0, nhb + h)),
            pl.BlockSpec((1, S, w), lambda b, h, i: (b, 0, 2 * nhb + h)),
            pl.BlockSpec((1, tile, LANES), lambda b, h, i: (b, i, 0)),
            pl.BlockSpec((1, hpb, nq, tile), lambda b, h, i: (b, h, 0, 0)),
        ],
        out_specs=pl.BlockSpec((1, tile, w), lambda b, h, i: (b, i, h)),
        out_shape=jax.ShapeDtypeStruct((B, S, heads * HEAD_DIM), BF16),
        compiler_params=_params(),
        name="fox_attn",
    )(qkv, qkv, qkv, ccol, crow4)


def _t5_bucket(dist):
    max_exact = NUM_BUCKETS // 2
    d = np.maximum(dist, 1).astype(np.float32)
    large = max_exact + (np.log(d / max_exact) / np.log(MAX_DISTANCE / max_exact)
                         * (NUM_BUCKETS - max_exact)).astype(np.int32)
    large = np.minimum(large, NUM_BUCKETS - 1)
    return np.where(dist < max_exact, dist, large).astype(np.int32)


def _dil_bias_tables(rel_bias, patterns, group_heads):
    period = 4 * DIL_TILE
    c = np.arange(period)
    back = c < 2 * DIL_TILE
    groups = []
    for g, (window, dil) in enumerate(patterns):
        assert window // dil == DIL_TILE
        strips = []
        for t in (0, 1):
            delta = np.where(back, t * DIL_TILE - c, t * DIL_TILE + period - c)
            valid = (delta >= 0) & (delta <= DIL_TILE) & (back | (c > period - DIL_TILE))
            bucket = _t5_bucket(np.clip(delta, 0, DIL_TILE) * dil)
            vec = rel_bias[bucket][:, g * group_heads:(g + 1) * group_heads].astype(F32)
            vec = jnp.where(valid[:, None], vec, NEG_INF).T
            flat = jnp.tile(vec, (1, DIL_TILE))[:, :DIL_TILE * (period - 1)]
            strips.append(flat.reshape(group_heads, DIL_TILE, period - 1)[:, :, :2 * DIL_TILE])
        groups.append(jnp.stack(strips, axis=1))
    return jnp.stack(groups)


def _dil_attn_kernel(*refs, dils, seq):
    n_g = len(dils)
    q_refs, k_refs, v_refs = refs[:n_g], refs[n_g:2 * n_g], refs[2 * n_g:3 * n_g]
    t_ref = refs[3 * n_g]
    o_refs = refs[3 * n_g + 1:4 * n_g + 1]
    o_sc, lse_sc = refs[4 * n_g + 1:]

    def rows(start, size, stride):
        return pl.ds(start, size) if stride == 1 else pl.ds(start, size, stride=stride)

    for g, d in enumerate(dils):
        cls_len = seq // d
        n_keys = min(2 * DIL_TILE, cls_len)
        for r in range(d):
            for u0 in range(0, cls_len, DIL_TILE):
                s0 = min(max(u0 - DIL_TILE, 0), cls_len - n_keys)
                strip = (u0 - s0) // DIL_TILE
                q_rows = rows(r + d * u0, DIL_TILE, d)
                k_rows = rows(r + d * s0, n_keys, d)
                q = q_refs[g][0, q_rows, :].astype(BF16)
                k = k_refs[g][0, k_rows, :].astype(BF16)
                v = v_refs[g][0, k_rows, :].astype(BF16)
                s = lax.dot_general(q, k, _NT, preferred_element_type=F32)
                s = s + t_ref[g, 0, strip][:, :n_keys]
                m = jnp.max(s, axis=1, keepdims=True)
                p = jnp.exp(s - m)
                l = jnp.sum(p, axis=1, keepdims=True)
                o = jnp.dot(p.astype(BF16), v, preferred_element_type=F32) / l
                o_sc[g, q_rows, :] = o
                lse_sc[g, q_rows, :] = jnp.broadcast_to(m + jnp.log(l), (DIL_TILE, LANES))

    lse = [lse_sc[g] for g in range(n_g)]
    mx = functools.reduce(jnp.maximum, lse)
    e = [jnp.exp(x - mx) for x in lse]
    inv = 1.0 / functools.reduce(lambda x, y: x + y, e)
    for g in range(n_g):
        o_refs[g][0] = (o_sc[g] * (e[g] * inv)).astype(o_refs[g].dtype)


def _dil_attention(qkv, tables, *, patterns, group_heads):
    B, S, _ = qkv.shape
    n_g = len(patterns)
    dils = tuple(d for _, d in patterns)
    assert all(S % (d * DIL_TILE) == 0 for d in dils)
    nh = n_g * group_heads

    def head_spec(part, g):
        return pl.BlockSpec((1, S, HEAD_DIM),
                            lambda b, h, part=part, g=g: (b, 0, part * nh + g * group_heads + h))

    in_specs = [head_spec(part, g) for part in range(3) for g in range(n_g)]
    in_specs.append(pl.BlockSpec((n_g, 1, 2, DIL_TILE, 2 * DIL_TILE),
                                 lambda b, h: (0, h, 0, 0, 0)))
    out_spec = pl.BlockSpec((1, S, HEAD_DIM), lambda b, h: (b, 0, h))
    return pl.pallas_call(
        functools.partial(_dil_attn_kernel, dils=dils, seq=S),
        grid=(B, group_heads),
        in_specs=in_specs,
        out_specs=[out_spec] * n_g,
        out_shape=[jax.ShapeDtypeStruct((B, S, group_heads * HEAD_DIM), BF16)] * n_g,
        scratch_shapes=[pltpu.VMEM((n_g, S, HEAD_DIM), F32), pltpu.VMEM((n_g, S, LANES), F32)],
        compiler_params=_params(),
        name="dil_attn",
    )(*([qkv] * (3 * n_g)), tables)


def _mix_cross_kernel(*refs, n_a, heads, scale):
    a_refs, wm_refs = refs[:n_a], refs[n_a:2 * n_a]
    x_ref, g_ref, wq_ref, kv_ref, wo_ref, o_ref = refs[2 * n_a:]
    x = x_ref[0]
    for a_ref, wm_ref in zip(a_refs, wm_refs):
        x = x + jnp.dot(a_ref[0], wm_ref[...], preferred_element_type=F32)
    h = _rmsnorm(x, g_ref[...]).astype(BF16)
    q = (jnp.dot(h, wq_ref[...].astype(BF16), preferred_element_type=F32) * scale).astype(BF16)
    kv = kv_ref[0]
    outs = []
    for hd in range(heads):
        qh = q[:, hd * HEAD_DIM:(hd + 1) * HEAD_DIM]
        kh = kv[:, hd * HEAD_DIM:(hd + 1) * HEAD_DIM]
        vh = kv[:, (heads + hd) * HEAD_DIM:(heads + hd + 1) * HEAD_DIM]
        s = lax.dot_general(qh, kh, _NT, preferred_element_type=F32)
        p = jnp.exp(s - jnp.max(s, axis=1, keepdims=True))
        l = jnp.sum(p, axis=1, keepdims=True)
        outs.append((jnp.dot(p.astype(BF16), vh, preferred_element_type=F32) / l).astype(BF16))
    o = jnp.concatenate(outs, axis=1)
    o_ref[0] = x + jnp.dot(o, wo_ref[...].astype(BF16), preferred_element_type=F32)


def _mix_cross(x, a_list, w_mix, mix_layer, gain, wq, kv, wo, layer, *, heads, tm):
    B, S, D = x.shape
    M = kv.shape[1]
    hd = heads * HEAD_DIM
    kg = a_list[0].shape[2]
    n_a = len(a_list)
    assert S % tm == 0 and kg * n_a == w_mix.shape[1]
    once = dict(pipeline_mode=pl.Buffered(1))
    in_specs = [pl.BlockSpec((1, tm, kg), lambda b, i: (b, i, 0)) for _ in a_list]
    in_specs += [pl.BlockSpec((None, kg, D), lambda b, i, g=g: (mix_layer, g, 0), **once)
                 for g in range(n_a)]
    in_specs += [
        pl.BlockSpec((1, tm, D), lambda b, i: (b, i, 0)),
        pl.BlockSpec((1, D), lambda b, i: (0, 0)),
        pl.BlockSpec((None, D, hd), lambda b, i: (layer, 0, 0), **once),
        pl.BlockSpec((1, M, 2 * hd), lambda b, i: (b, 0, 0)),
        pl.BlockSpec((None, hd, D), lambda b, i: (layer, 0, 0), **once),
    ]
    return pl.pallas_call(
        functools.partial(_mix_cross_kernel, n_a=n_a, heads=heads, scale=HEAD_DIM ** -0.5),
        grid=(B, S // tm),
        in_specs=in_specs,
        out_specs=pl.BlockSpec((1, tm, D), lambda b, i: (b, i, 0)),
        out_shape=jax.ShapeDtypeStruct((B, S, D), F32),
        compiler_params=_params(),
        name="mix_cross",
    )(*a_list, *([w_mix] * n_a), x, gain.reshape(1, D), wq, kv, wo)


def _fox_mixer(x, gain, w_in, w_gate, b_f, layer, *, heads, attn_tile, tm):
    B, S, D = x.shape
    hd = heads * HEAD_DIM
    w_f = jnp.pad(w_gate, ((0, 0), (0, LANES - heads))).astype(BF16)
    b_pad = jnp.pad(b_f, (0, LANES - heads)).reshape(1, LANES).astype(F32)
    qkv, logf = _rms_proj(x.reshape(B * S, D), gain, w_in, layer, n_out=3 * hd, tm=tm,
                          tn=min(1024, hd), out_dtype=BF16, q_cols=hd,
                          q_scale=HEAD_DIM ** -0.5 * LOG2E, gate=(w_f, b_pad))
    ccol, crow = _fox_cumsum(logf.reshape(B, S, LANES), heads)
    return [_fox_attention(qkv.reshape(B, S, 3 * hd), ccol, crow, heads=heads, tile=attn_tile)]


def _dilated_mixer(x, gain, w_in, layer, rel_bias, *, patterns, group_heads, tm):
    B, S, D = x.shape
    n_g = len(patterns)
    hd = n_g * group_heads * HEAD_DIM
    qkv = _rms_proj(x.reshape(B * S, D), gain, w_in, layer, n_out=3 * hd, tm=tm,
                    tn=group_heads * HEAD_DIM, out_dtype=F32, q_cols=hd, q_scale=HEAD_DIM ** -0.5)
    tables = _dil_bias_tables(rel_bias, patterns, group_heads)
    return _dil_attention(qkv.reshape(B, S, 3 * hd), tables, patterns=patterns,
                          group_heads=group_heads)


def _forward(x, mem, ffn1_norm, ffn1_w_in, ffn1_w_out, mix_norm, fox_w_in, fox_b_f, fox_w_out,
             dil_w_in, dil_w_out, rel_bias, cross_norm, mem_norm, cross_w_q, cross_w_kv,
             cross_w_out, ffn2_norm, ffn2_w_in, ffn2_w_out, final_norm, *,
             patterns, fox_heads, group_heads, cross_heads, tm, tf, head_tm, head_tf, proj_tm,
             attn_tile, cross_tm):
    B, S, D = x.shape
    M = mem.shape[1]
    depth = ffn1_norm.shape[0]
    n_mixers = 2
    fox_hd = fox_heads * HEAD_DIM
    cross_hd = cross_heads * HEAD_DIM
    fox_w_gate = fox_w_in[:, :, 3 * fox_hd:]
    fox_w_in, fox_w_out, dil_w_in, dil_w_out, cross_w_kv = (
        w.astype(BF16) for w in (fox_w_in, fox_w_out, dil_w_in, dil_w_out, cross_w_kv))
    ffn_tiles = dict(tm=tm, tf=tf, head_tm=head_tm, head_tf=head_tf)
    for i in range(depth):
        x = _ffn(x.reshape(B * S, D), ffn1_norm[i], ffn1_w_in, ffn1_w_out, i,
                 **ffn_tiles).reshape(B, S, D)
        j = i // n_mixers
        if i % n_mixers == 0:
            mixed = _fox_mixer(x, mix_norm[i], fox_w_in, fox_w_gate[j], fox_b_f[j], j,
                               heads=fox_heads, attn_tile=attn_tile, tm=proj_tm)
            w_mix = fox_w_out
        else:
            mixed = _dilated_mixer(x, mix_norm[i], dil_w_in, j, rel_bias,
                                   patterns=patterns, group_heads=group_heads, tm=proj_tm)
            w_mix = dil_w_out
        kv = _rms_proj(mem.reshape(B * M, D), mem_norm, cross_w_kv, i, n_out=2 * cross_hd,
                       tm=min(tm, B * M), tn=min(512, cross_hd), out_dtype=BF16)
        x = _mix_cross(x, mixed, w_mix, j, cross_norm[i], cross_w_q,
                       kv.reshape(B, M, 2 * cross_hd), cross_w_out, i,
                       heads=cross_heads, tm=cross_tm)
        x = _ffn(x.reshape(B * S, D), ffn2_norm[i], ffn2_w_in, ffn2_w_out, i, **ffn_tiles,
                 final_gain=final_norm if i == depth - 1 else None).reshape(B, S, D)
    return x


def kernel(x, mem, ffn1_norm, ffn1_w_in, ffn1_w_out, mix_norm, fox_w_in, fox_b_f, fox_w_out, dil_w_in, dil_w_out, rel_bias, cross_norm, mem_norm, cross_w_q, cross_w_kv, cross_w_out, ffn2_norm, ffn2_w_in, ffn2_w_out, final_norm):
    return _forward(x, mem, ffn1_norm, ffn1_w_in, ffn1_w_out, mix_norm, fox_w_in, fox_b_f,
                    fox_w_out, dil_w_in, dil_w_out, rel_bias, cross_norm, mem_norm, cross_w_q,
                    cross_w_kv, cross_w_out, ffn2_norm, ffn2_w_in, ffn2_w_out, final_norm,
                    patterns=DIL_PATTERNS, fox_heads=FOX_HEADS, group_heads=DIL_GROUP_HEADS,
                    cross_heads=CROSS_HEADS, tm=512, tf=512, head_tm=1024, head_tf=256,
                    proj_tm=1024, attn_tile=512, cross_tm=512)
```

```python
import functools

import numpy as np
import jax
import jax.numpy as jnp
from jax import lax
from jax.experimental import pallas as pl
from jax.experimental.pallas import tpu as pltpu

HEAD_DIM = 128
FOX_HEADS = 16
DIL_PATTERNS = ((128, 1), (512, 4), (2048, 16))
DIL_GROUP_HEADS = 6
CROSS_HEADS = 4
NUM_BUCKETS = 32
MAX_DISTANCE = 2048
RMS_EPS = 1e-6
NEG_INF = -1e30

LANES = 128
DIL_TILE = 128
VMEM_LIMIT_BYTES = 56 * 1024 * 1024

BF16 = jnp.bfloat16
F32 = jnp.float32
_NT = (((1,), (1,)), ((), ()))
LOG2E = 1.4426950408889634


def _params():
    return pltpu.CompilerParams(vmem_limit_bytes=VMEM_LIMIT_BYTES)


def _rmsnorm(x, g):
    return x * lax.rsqrt(jnp.mean(x * x, axis=-1, keepdims=True) + RMS_EPS) * g


def _log_sigmoid(z):
    return jnp.minimum(z, 0.0) - jnp.log1p(jnp.exp(-jnp.abs(z)))


def _early_next_tile(n_tiles, switch=1):
    return lambda i, j: (jnp.where(j < switch, i, jnp.minimum(i + 1, n_tiles - 1)), 0)


def _ahead_chunks(steps, tm):
    assert steps >= 1
    return max(c for c in (1, 2, 4, 8) if c <= steps and tm % (16 * c) == 0)


def _split3(x):
    hi = x.astype(BF16)
    r = x - hi.astype(F32)
    mid = r.astype(BF16)
    lo = (r - mid.astype(F32)).astype(BF16)
    return hi, mid, lo


def _norm_ahead(x_ref, g_ref, h_ref, slot, chunk, ahead_rows):
    rows = pl.ds(pl.multiple_of(chunk * ahead_rows, ahead_rows), ahead_rows)
    h_ref[1 - slot, rows, :] = _rmsnorm(x_ref[rows, :], g_ref[...]).astype(BF16)


def _rms_proj_kernel(x_ref, g_ref, w_ref, *rest, tn, q_cols, q_scale, with_gate, ahead_rows):
    if with_gate:
        wf_ref, bf_ref, o_ref, logf_ref, h_ref = rest
    else:
        o_ref, h_ref = rest
    i = pl.program_id(0)
    j = pl.program_id(1)
    slot = lax.rem(i, 2) if ahead_rows else 0

    @pl.when(j == 0)
    def _():
        def norm_now():
            h_ref[slot] = _rmsnorm(x_ref[...], g_ref[...]).astype(BF16)

        if ahead_rows:
            pl.when(i == 0)(norm_now)
        else:
            norm_now()

        if with_gate:
            z = jnp.dot(h_ref[slot], wf_ref[...], preferred_element_type=F32) + bf_ref[...]
            logf_ref[...] = _log_sigmoid(z)

    acc = jnp.dot(h_ref[slot], w_ref[...], preferred_element_type=F32)
    if q_cols:
        acc = acc * jnp.where(j * tn < q_cols, q_scale, 1.0).astype(F32)
    o_ref[...] = acc.astype(o_ref.dtype)
    if ahead_rows:
        _norm_ahead(x_ref, g_ref, h_ref, slot,
                    jnp.clip(j - 1, 0, x_ref.shape[0] // ahead_rows - 1), ahead_rows)


def _rms_proj(x, gain, w, layer, *, n_out, tm, tn, out_dtype, q_cols=0, q_scale=1.0, gate=None):
    T, D = x.shape
    assert T % tm == 0 and n_out % tn == 0 and q_cols % tn == 0 and n_out <= w.shape[2]
    steps = n_out // tn - 1
    ahead_rows = tm // 8 if steps >= 8 and tm % LANES == 0 else 0
    in_specs = [
        pl.BlockSpec((tm, D), _early_next_tile(T // tm)),
        pl.BlockSpec((1, D), lambda i, j: (0, 0)),
        pl.BlockSpec((None, D, tn), lambda i, j: (layer, 0, j)),
    ]
    args = [x, gain.reshape(1, D), w]
    out_shape = [jax.ShapeDtypeStruct((T, n_out), out_dtype)]
    out_specs = [pl.BlockSpec((tm, tn), lambda i, j: (i, j))]
    if gate is not None:
        in_specs += [pl.BlockSpec((D, LANES), lambda i, j: (0, 0)),
                     pl.BlockSpec((1, LANES), lambda i, j: (0, 0))]
        args += list(gate)
        out_shape.append(jax.ShapeDtypeStruct((T, LANES), F32))
        out_specs.append(pl.BlockSpec((tm, LANES), lambda i, j: (i, 0)))
    res = pl.pallas_call(
        functools.partial(_rms_proj_kernel, tn=tn, q_cols=q_cols, q_scale=q_scale,
                          with_gate=gate is not None, ahead_rows=ahead_rows),
        grid=(T // tm, n_out // tn),
        in_specs=in_specs,
        out_specs=out_specs,
        out_shape=out_shape,
        scratch_shapes=[pltpu.VMEM((2 if ahead_rows else 1, tm, D), BF16)],
        compiler_params=_params(),
        name="rms_proj_gate" if gate is not None else "rms_proj",
    )(*args)
    return res if gate is not None else res[0]


def _ffn_start(x_ref, g_ref, h_ref, o_ref):
    x = x_ref[...]
    h_ref[...] = _rmsnorm(x, g_ref[...]).astype(BF16)
    o_ref[...] = x


def _ffn_chunk(h_ref, base_ref, o_ref, wg, wu, wo):
    h = h_ref[...]
    gate = jnp.dot(h, wg, preferred_element_type=F32)
    up = jnp.dot(h, wu, preferred_element_type=F32)
    act = (gate * jax.nn.sigmoid(gate) * (0.5 * up)).astype(BF16)
    o_ref[...] = base_ref[...] + jnp.dot(act, wo, preferred_element_type=F32)


def _ffn_head_kernel(x_ref, g_ref, wg_ref, wu_ref, wo_ref, *rest, final_norm):
    if final_norm:
        fg_ref, o_ref, wg_out, wu_out, wo_out, h_ref = rest
    else:
        o_ref, wg_out, wu_out, wo_out, h_ref = rest
    j = pl.program_id(0)

    @pl.when(j == 0)
    def _():
        _ffn_start(x_ref, g_ref, h_ref, o_ref)

    wg, wu, wo = (w[...].astype(BF16) for w in (wg_ref, wu_ref, wo_ref))
    wg_out[...] = wg
    wu_out[...] = wu
    wo_out[...] = wo
    _ffn_chunk(h_ref, o_ref, o_ref, wg, wu, wo)

    if final_norm:
        @pl.when(j == pl.num_programs(0) - 1)
        def _():
            o_ref[...] = _rmsnorm(o_ref[...], fg_ref[...])


def _ffn_main_kernel(x_ref, g_ref, wg_ref, wu_ref, wo_ref, head_ref, *rest, final_norm, n_head,
                     switch, ahead_rows):
    if final_norm:
        fg_ref, o_ref, h_ref = rest
    else:
        o_ref, h_ref = rest
    i = pl.program_id(0)
    j = pl.program_id(1)
    last = pl.num_programs(1) - 1
    slot = lax.rem(i, 2)
    n_ahead = x_ref.shape[0] // ahead_rows

    @pl.when(jnp.logical_and(i < n_head, j == last))
    def _():
        o_ref[...] = head_ref[...]

    @pl.when(i >= n_head)
    def _():
        @pl.when(j == 0)
        def _():
            @pl.when(i == n_head)
            def _():
                h_ref[slot] = _rmsnorm(x_ref[...], g_ref[...]).astype(BF16)

            _ffn_chunk(h_ref.at[slot], x_ref, o_ref, wg_ref[...], wu_ref[...], wo_ref[...])

        @pl.when(j > 0)
        def _():
            _ffn_chunk(h_ref.at[slot], o_ref, o_ref, wg_ref[...], wu_ref[...], wo_ref[...])
            _norm_ahead(x_ref, g_ref, h_ref, slot, jnp.clip(j - switch, 0, n_ahead - 1),
                        ahead_rows)

        if final_norm:
            @pl.when(j == last)
            def _():
                o_ref[...] = _rmsnorm(o_ref[...], fg_ref[...])


def _ffn(x, gain, w_in, w_out, layer, *, tm, tf, head_tm, head_tf, final_gain=None):
    T, D = x.shape
    F = w_out.shape[1]
    assert T % tm == 0 and F % tf == 0 and F % head_tf == 0
    assert head_tm % tm == 0 and head_tm <= T
    final = final_gain is not None
    gain2 = gain.reshape(1, D)
    extra_specs = [pl.BlockSpec((1, D), lambda *_: (0, 0))] if final else []
    extra_args = [final_gain.reshape(1, D)] if final else []

    nfh = F // head_tf
    per = tf // head_tf
    assert tf % head_tf == 0
    once = dict(pipeline_mode=pl.Buffered(1))
    head, wg, wu, wo = pl.pallas_call(
        functools.partial(_ffn_head_kernel, final_norm=final),
        grid=(nfh,),
        in_specs=[
            pl.BlockSpec((head_tm, D), lambda j: (0, 0), **once),
            pl.BlockSpec((1, D), lambda j: (0, 0)),
            pl.BlockSpec((None, D, head_tf), lambda j: (layer, 0, j)),
            pl.BlockSpec((None, D, head_tf), lambda j: (layer, 0, j + nfh)),
            pl.BlockSpec((None, head_tf, D), lambda j: (layer, j, 0)),
        ] + extra_specs,
        out_specs=[
            pl.BlockSpec((head_tm, D), lambda j: (0, 0), **once),
            pl.BlockSpec((None, D, head_tf), lambda j: (j // per, 0, j % per)),
            pl.BlockSpec((None, D, head_tf), lambda j: (j // per, 0, j % per)),
            pl.BlockSpec((head_tf, D), lambda j: (j, 0)),
        ],
        out_shape=[
            jax.ShapeDtypeStruct((head_tm, D), F32),
            jax.ShapeDtypeStruct((F // tf, D, tf), BF16),
            jax.ShapeDtypeStruct((F // tf, D, tf), BF16),
            jax.ShapeDtypeStruct((F, D), BF16),
        ],
        scratch_shapes=[pltpu.VMEM((head_tm, D), BF16)],
        compiler_params=_params(),
        name="ffn_head_final" if final else "ffn_head",
    )(x, gain2, w_in, w_in, w_out, *extra_args)

    n_head = head_tm // tm
    nf = F // tf
    switch = nf // 2
    n_ahead = _ahead_chunks(nf - switch, tm)

    def chunk(i, j):
        return jnp.where(i < n_head, 0, j)

    return pl.pallas_call(
        functools.partial(_ffn_main_kernel, final_norm=final, n_head=n_head, switch=switch,
                          ahead_rows=tm // n_ahead),
        grid=(T // tm, nf),
        in_specs=[
            pl.BlockSpec((tm, D), _early_next_tile(T // tm, switch=switch)),
            pl.BlockSpec((1, D), lambda i, j: (0, 0)),
            pl.BlockSpec((None, D, tf), lambda i, j: (chunk(i, j), 0, 0)),
            pl.BlockSpec((None, D, tf), lambda i, j: (chunk(i, j), 0, 0)),
            pl.BlockSpec((tf, D), lambda i, j: (chunk(i, j), 0)),
            pl.BlockSpec((tm, D), lambda i, j: (jnp.minimum(i, n_head - 1), 0)),
        ] + extra_specs,
        out_specs=pl.BlockSpec((tm, D), lambda i, j: (i, 0)),
        out_shape=jax.ShapeDtypeStruct((T, D), F32),
        scratch_shapes=[pltpu.VMEM((2, tm, D), BF16)],
        compiler_params=_params(),
        name="ffn_final" if final else "ffn",
    )(x, gain2, wg, wu, wo, head, *extra_args)


def _cumsum_kernel(tri_ref, x_ref, ccol_ref, crow_ref, *, heads):
    tri = tri_ref[...]
    blk = tri.shape[0]
    carry = jnp.zeros((1, LANES), F32)
    for r0 in range(0, x_ref.shape[1], blk):
        pieces = _split3(x_ref[0, r0:r0 + blk, :] * LOG2E)
        c = carry + sum(jnp.dot(tri, p, preferred_element_type=F32) for p in pieces)
        ccol_ref[0, r0:r0 + blk, :] = c
        carry = c[blk - 1:blk, :]
    crow_ref[0] = ccol_ref[0].T[:heads]


def _fox_cumsum(logf, heads, blk=256):
    B, S, _ = logf.shape
    blk = min(blk, S)
    assert S % blk == 0
    tri = jnp.tril(jnp.ones((blk, blk), BF16))
    return pl.pallas_call(
        functools.partial(_cumsum_kernel, heads=heads),
        grid=(B,),
        in_specs=[pl.BlockSpec((blk, blk), lambda b: (0, 0)),
                  pl.BlockSpec((1, S, LANES), lambda b: (b, 0, 0))],
        out_specs=[pl.BlockSpec((1, S, LANES), lambda b: (b, 0, 0)),
                   pl.BlockSpec((1, heads, S), lambda b: (b, 0, 0))],
        out_shape=[jax.ShapeDtypeStruct((B, S, LANES), F32),
                   jax.ShapeDtypeStruct((B, heads, S), F32)],
        compiler_params=_params(),
        name="fox_cumsum",
    )(tri, logf)


def _fox_attn_kernel(q_ref, k_ref, v_ref, ccol_ref, crow_ref, o_ref, *, tile, hpb):
    hb = pl.program_id(1)
    qi = pl.program_id(2)
    lane = lax.broadcasted_iota(jnp.int32, (tile, LANES), 1)
    ccol = ccol_ref[0]
    heads = range(hpb)
    q = [q_ref[0, :, e * HEAD_DIM:(e + 1) * HEAD_DIM] for e in heads]
    cq = [jnp.sum(jnp.where(lane == hb * hpb + e, ccol, 0.0), axis=1, keepdims=True)
          for e in heads]

    def step(j, carry, diagonal):
        start = pl.multiple_of(j * tile, tile)
        out = []
        for e in heads:
            m, l, acc = carry[e]
            k = k_ref[0, pl.ds(start, tile), e * HEAD_DIM:(e + 1) * HEAD_DIM]
            v = v_ref[0, pl.ds(start, tile), e * HEAD_DIM:(e + 1) * HEAD_DIM]
            ck = crow_ref[0, e, pl.ds(j, 1), :]
            s = lax.dot_general(q[e], k, _NT, preferred_element_type=F32) - ck
            if diagonal:
                row = lax.broadcasted_iota(jnp.int32, (tile, tile), 0)
                col = lax.broadcasted_iota(jnp.int32, (tile, tile), 1)
                s = jnp.where(col <= row, s, NEG_INF)
            m_new = jnp.maximum(m, jnp.max(s, axis=1, keepdims=True) + cq[e])
            alpha = jnp.exp2(m - m_new)
            p = jnp.exp2(s - (m_new - cq[e]))
            l = alpha * l + jnp.sum(p, axis=1, keepdims=True)
            acc = alpha * acc + jnp.dot(p.astype(BF16), v, preferred_element_type=F32)
            out.append((m_new, l, acc))
        return tuple(out)

    init = tuple((jnp.full((tile, 1), NEG_INF, F32), jnp.zeros((tile, 1), F32),
                  jnp.zeros((tile, HEAD_DIM), F32)) for _ in heads)
    carry = lax.fori_loop(0, qi, lambda j, c: step(j, c, False), init)
    carry = step(qi, carry, True)
    for e in heads:
        _, l, acc = carry[e]
        o_ref[0, :, e * HEAD_DIM:(e + 1) * HEAD_DIM] = (acc / l).astype(o_ref.dtype)


def _fox_attention(qkv, ccol, crow, *, heads, tile, hpb=4):
    B, S, _ = qkv.shape
    hpb = min(hpb, heads)
    assert S % tile == 0 and heads % hpb == 0
    nq = S // tile
    nhb = heads // hpb
    w = hpb * HEAD_DIM
    crow4 = crow.reshape(B, heads, nq, tile)
    return pl.pallas_call(
        functools.partial(_fox_attn_kernel, tile=tile, hpb=hpb),
        grid=(B, nhb, nq),
        in_specs=[
            pl.BlockSpec((1, tile, w), lambda b, h, i: (b, i, h)),
            pl.BlockSpec((1, S, w), lambda b, h, i: (b, 0, nhb + h)),
            pl.BlockSpec((1, S, w), lambda b, h, i: (b, 0, 2 * nhb + h)),
            pl.BlockSpec((1, tile, LANES), lambda b, h, i: (b, i, 0)),
            pl.BlockSpec((1, hpb, nq, tile), lambda b, h, i: (b, h, 0, 0)),
        ],
        out_specs=pl.BlockSpec((1, tile, w), lambda b, h, i: (b, i, h)),
        out_shape=jax.ShapeDtypeStruct((B, S, heads * HEAD_DIM), BF16),
        compiler_params=_params(),
        name="fox_attn",
    )(qkv, qkv, qkv, ccol, crow4)


def _t5_bucket(dist):
    max_exact = NUM_BUCKETS // 2
    d = np.maximum(dist, 1).astype(np.float32)
    large = max_exact + (np.log(d / max_exact) / np.log(MAX_DISTANCE / max_exact)
                         * (NUM_BUCKETS - max_exact)).astype(np.int32)
    large = np.minimum(large, NUM_BUCKETS - 1)
    return np.where(dist < max_exact, dist, large).astype(np.int32)


def _dil_bias_tables(rel_bias, patterns, group_heads):
    period = 4 * DIL_TILE
    c = np.arange(period)
    back = c < 2 * DIL_TILE
    groups = []
    for g, (window, dil) in enumerate(patterns):
        assert window // dil == DIL_TILE
        strips = []
        for t in (0, 1):
            delta = np.where(back, t * DIL_TILE - c, t * DIL_TILE + period - c)
            valid = (delta >= 0) & (delta <= DIL_TILE) & (back | (c > period - DIL_TILE))
            bucket = _t5_bucket(np.clip(delta, 0, DIL_TILE) * dil)
            vec = rel_bias[bucket][:, g * group_heads:(g + 1) * group_heads].astype(F32)
            vec = jnp.where(valid[:, None], vec, NEG_INF).T
            flat = jnp.tile(vec, (1, DIL_TILE))[:, :DIL_TILE * (period - 1)]
            strips.append(flat.reshape(group_heads, DIL_TILE, period - 1)[:, :, :2 * DIL_TILE])
        groups.append(jnp.stack(strips, axis=1))
    return jnp.stack(groups)


def _dil_attn_kernel(*refs, dils, seq):
    n_g = len(dils)
    q_refs, k_refs, v_refs = refs[:n_g], refs[n_g:2 * n_g], refs[2 * n_g:3 * n_g]
    t_ref = refs[3 * n_g]
    o_refs = refs[3 * n_g + 1:4 * n_g + 1]
    o_sc, lse_sc = refs[4 * n_g + 1:]

    def rows(start, size, stride):
        return pl.ds(start, size) if stride == 1 else pl.ds(start, size, stride=stride)

    for g, d in enumerate(dils):
        cls_len = seq // d
        n_keys = min(2 * DIL_TILE, cls_len)
        for r in range(d):
            for u0 in range(0, cls_len, DIL_TILE):
                s0 = min(max(u0 - DIL_TILE, 0), cls_len - n_keys)
                strip = (u0 - s0) // DIL_TILE
                q_rows = rows(r + d * u0, DIL_TILE, d)
                k_rows = rows(r + d * s0, n_keys, d)
                q = q_refs[g][0, q_rows, :].astype(BF16)
                k = k_refs[g][0, k_rows, :].astype(BF16)
                v = v_refs[g][0, k_rows, :].astype(BF16)
                s = lax.dot_general(q, k, _NT, preferred_element_type=F32)
                s = s + t_ref[g, 0, strip][:, :n_keys]
                m = jnp.max(s, axis=1, keepdims=True)
                p = jnp.exp(s - m)
                l = jnp.sum(p, axis=1, keepdims=True)
                o = jnp.dot(p.astype(BF16), v, preferred_element_type=F32) / l
                o_sc[g, q_rows, :] = o
                lse_sc[g, q_rows, :] = jnp.broadcast_to(m + jnp.log(l), (DIL_TILE, LANES))

    lse = [lse_sc[g] for g in range(n_g)]
    mx = functools.reduce(jnp.maximum, lse)
    e = [jnp.exp(x - mx) for x in lse]
    inv = 1.0 / functools.reduce(lambda x, y: x + y, e)
    for g in range(n_g):
        o_refs[g][0] = (o_sc[g] * (e[g] * inv)).astype(o_refs[g].dtype)


def _dil_attention(qkv, tables, *, patterns, group_heads):
    B, S, _ = qkv.shape
    n_g = len(patterns)
    dils = tuple(d for _, d in patterns)
    assert all(S % (d * DIL_TILE) == 0 for d in dils)
    nh = n_g * group_heads

    def head_spec(part, g):
        return pl.BlockSpec((1, S, HEAD_DIM),
                            lambda b, h, part=part, g=g: (b, 0, part * nh + g * group_heads + h))

    in_specs = [head_spec(part, g) for part in range(3) for g in range(n_g)]
    in_specs.append(pl.BlockSpec((n_g, 1, 2, DIL_TILE, 2 * DIL_TILE),
                                 lambda b, h: (0, h, 0, 0, 0)))
    out_spec = pl.BlockSpec((1, S, HEAD_DIM), lambda b, h: (b, 0, h))
    return pl.pallas_call(
        functools.partial(_dil_attn_kernel, dils=dils, seq=S),
        grid=(B, group_heads),
        in_specs=in_specs,
        out_specs=[out_spec] * n_g,
        out_shape=[jax.ShapeDtypeStruct((B, S, group_heads * HEAD_DIM), BF16)] * n_g,
        scratch_shapes=[pltpu.VMEM((n_g, S, HEAD_DIM), F32), pltpu.VMEM((n_g, S, LANES), F32)],
        compiler_params=_params(),
        name="dil_attn",
    )(*([qkv] * (3 * n_g)), tables)


def _mix_cross_kernel(*refs, n_a, heads, scale):
    a_refs, wm_refs = refs[:n_a], refs[n_a:2 * n_a]
    x_ref, g_ref, wq_ref, kv_ref, wo_ref, o_ref = refs[2 * n_a:]
    x = x_ref[0]
    for a_ref, wm_ref in zip(a_refs, wm_refs):
        x = x + jnp.dot(a_ref[0], wm_ref[...], preferred_element_type=F32)
    h = _rmsnorm(x, g_ref[...]).astype(BF16)
    q = (jnp.dot(h, wq_ref[...].astype(BF16), preferred_element_type=F32) * scale).astype(BF16)
    kv = kv_ref[0]
    outs = []
    for hd in range(heads):
        qh = q[:, hd * HEAD_DIM:(hd + 1) * HEAD_DIM]
        kh = kv[:, hd * HEAD_DIM:(hd + 1) * HEAD_DIM]
        vh = kv[:, (heads + hd) * HEAD_DIM:(heads + hd + 1) * HEAD_DIM]
        s = lax.dot_general(qh, kh, _NT, preferred_element_type=F32)
        p = jnp.exp(s - jnp.max(s, axis=1, keepdims=True))
        l = jnp.sum(p, axis=1, keepdims=True)
        outs.append((jnp.dot(p.astype(BF16), vh, preferred_element_type=F32) / l).astype(BF16))
    o = jnp.concatenate(outs, axis=1)
    o_ref[0] = x + jnp.dot(o, wo_ref[...].astype(BF16), preferred_element_type=F32)


def _mix_cross(x, a_list, w_mix, mix_layer, gain, wq, kv, wo, layer, *, heads, tm):
    B, S, D = x.shape
    M = kv.shape[1]
    hd = heads * HEAD_DIM
    kg = a_list[0].shape[2]
    n_a = len(a_list)
    assert S % tm == 0 and kg * n_a == w_mix.shape[1]
    once = dict(pipeline_mode=pl.Buffered(1))
    in_specs = [pl.BlockSpec((1, tm, kg), lambda b, i: (b, i, 0)) for _ in a_list]
    in_specs += [pl.BlockSpec((None, kg, D), lambda b, i, g=g: (mix_layer, g, 0), **once)
                 for g in range(n_a)]
    in_specs += [
        pl.BlockSpec((1, tm, D), lambda b, i: (b, i, 0)),
        pl.BlockSpec((1, D), lambda b, i: (0, 0)),
        pl.BlockSpec((None, D, hd), lambda b, i: (layer, 0, 0), **once),
        pl.BlockSpec((1, M, 2 * hd), lambda b, i: (b, 0, 0)),
        pl.BlockSpec((None, hd, D), lambda b, i: (layer, 0, 0), **once),
    ]
    return pl.pallas_call(
        functools.partial(_mix_cross_kernel, n_a=n_a, heads=heads, scale=HEAD_DIM ** -0.5),
        grid=(B, S // tm),
        in_specs=in_specs,
        out_specs=pl.BlockSpec((1, tm, D), lambda b, i: (b, i, 0)),
        out_shape=jax.ShapeDtypeStruct((B, S, D), F32),
        compiler_params=_params(),
        name="mix_cross",
    )(*a_list, *([w_mix] * n_a), x, gain.reshape(1, D), wq, kv, wo)


def _fox_mixer(x, gain, w_in, w_gate, b_f, layer, *, heads, attn_tile, tm):
    B, S, D = x.shape
    hd = heads * HEAD_DIM
    w_f = jnp.pad(w_gate, ((0, 0), (0, LANES - heads))).astype(BF16)
    b_pad = jnp.pad(b_f, (0, LANES - heads)).reshape(1, LANES).astype(F32)
    qkv, logf = _rms_proj(x.reshape(B * S, D), gain, w_in, layer, n_out=3 * hd, tm=tm,
                          tn=min(1024, hd), out_dtype=BF16, q_cols=hd,
                          q_scale=HEAD_DIM ** -0.5 * LOG2E, gate=(w_f, b_pad))
    ccol, crow = _fox_cumsum(logf.reshape(B, S, LANES), heads)
    return [_fox_attention(qkv.reshape(B, S, 3 * hd), ccol, crow, heads=heads, tile=attn_tile)]


def _dilated_mixer(x, gain, w_in, layer, rel_bias, *, patterns, group_heads, tm):
    B, S, D = x.shape
    n_g = len(patterns)
    hd = n_g * group_heads * HEAD_DIM
    qkv = _rms_proj(x.reshape(B * S, D), gain, w_in, layer, n_out=3 * hd, tm=tm,
                    tn=group_heads * HEAD_DIM, out_dtype=F32, q_cols=hd, q_scale=HEAD_DIM ** -0.5)
    tables = _dil_bias_tables(rel_bias, patterns, group_heads)
    return _dil_attention(qkv.reshape(B, S, 3 * hd), tables, patterns=patterns,
                          group_heads=group_heads)


def _forward(x, mem, ffn1_norm, ffn1_w_in, ffn1_w_out, mix_norm, fox_w_in, fox_b_f, fox_w_out,
             dil_w_in, dil_w_out, rel_bias, cross_norm, mem_norm, cross_w_q, cross_w_kv,
             cross_w_out, ffn2_norm, ffn2_w_in, ffn2_w_out, final_norm, *,
             patterns, fox_heads, group_heads, cross_heads, tm, tf, head_tm, head_tf, proj_tm,
             attn_tile, cross_tm):
    B, S, D = x.shape
    M = mem.shape[1]
    depth = ffn1_norm.shape[0]
    n_mixers = 2
    fox_hd = fox_heads * HEAD_DIM
    cross_hd = cross_heads * HEAD_DIM
    fox_w_gate = fox_w_in[:, :, 3 * fox_hd:]
    fox_w_in, fox_w_out, dil_w_in, dil_w_out, cross_w_kv = (
        w.astype(BF16) for w in (fox_w_in, fox_w_out, dil_w_in, dil_w_out, cross_w_kv))
    ffn_tiles = dict(tm=tm, tf=tf, head_tm=head_tm, head_tf=head_tf)
    for i in range(depth):
        x = _ffn(x.reshape(B * S, D), ffn1_norm[i], ffn1_w_in, ffn1_w_out, i,
                 **ffn_tiles).reshape(B, S, D)
        j = i // n_mixers
        if i % n_mixers == 0:
            mixed = _fox_mixer(x, mix_norm[i], fox_w_in, fox_w_gate[j], fox_b_f[j], j,
                               heads=fox_heads, attn_tile=attn_tile, tm=proj_tm)
            w_mix = fox_w_out
        else:
            mixed = _dilated_mixer(x, mix_norm[i], dil_w_in, j, rel_bias,
                                   patterns=patterns, group_heads=group_heads, tm=proj_tm)
            w_mix = dil_w_out
        kv = _rms_proj(mem.reshape(B * M, D), mem_norm, cross_w_kv, i, n_out=2 * cross_hd,
                       tm=min(tm, B * M), tn=min(512, cross_hd), out_dtype=BF16)
        x = _mix_cross(x, mixed, w_mix, j, cross_norm[i], cross_w_q,
                       kv.reshape(B, M, 2 * cross_hd), cross_w_out, i,
                       heads=cross_heads, tm=cross_tm)
        x = _ffn(x.reshape(B * S, D), ffn2_norm[i], ffn2_w_in, ffn2_w_out, i, **ffn_tiles,
                 final_gain=final_norm if i == depth - 1 else None).reshape(B, S, D)
    return x


def kernel(x, mem, ffn1_norm, ffn1_w_in, ffn1_w_out, mix_norm, fox_w_in, fox_b_f, fox_w_out, dil_w_in, dil_w_out, rel_bias, cross_norm, mem_norm, cross_w_q, cross_w_kv, cross_w_out, ffn2_norm, ffn2_w_in, ffn2_w_out, final_norm):
    return _forward(x, mem, ffn1_norm, ffn1_w_in, ffn1_w_out, mix_norm, fox_w_in, fox_b_f,
                    fox_w_out, dil_w_in, dil_w_out, rel_bias, cross_norm, mem_norm, cross_w_q,
                    cross_w_kv, cross_w_out, ffn2_norm, ffn2_w_in, ffn2_w_out, final_norm,
                    patterns=DIL_PATTERNS, fox_heads=FOX_HEADS, group_heads=DIL_GROUP_HEADS,
                    cross_heads=CROSS_HEADS, tm=512, tf=512, head_tm=1024, head_tf=256,
                    proj_tm=1024, attn_tile=512, cross_tm=512)
```

```python
import functools

import numpy as np
import jax
import jax.numpy as jnp
from jax import lax
from jax.experimental import pallas as pl
from jax.experimental.pallas import tpu as pltpu

HEAD_DIM = 128
FOX_HEADS = 16
DIL_PATTERNS = ((128, 1), (512, 4), (2048, 16))
DIL_GROUP_HEADS = 6
CROSS_HEADS = 4
NUM_BUCKETS = 32
MAX_DISTANCE = 2048
RMS_EPS = 1e-6
NEG_INF = -1e30

LANES = 128
DIL_TILE = 128
VMEM_LIMIT_BYTES = 56 * 1024 * 1024

BF16 = jnp.bfloat16
F32 = jnp.float32
_NT = (((1,), (1,)), ((), ()))
LOG2E = 1.4426950408889634


def _params():
    return pltpu.CompilerParams(vmem_limit_bytes=VMEM_LIMIT_BYTES)


def _rmsnorm(x, g):
    return x * lax.rsqrt(jnp.mean(x * x, axis=-1, keepdims=True) + RMS_EPS) * g


def _log_sigmoid(z):
    return jnp.minimum(z, 0.0) - jnp.log1p(jnp.exp(-jnp.abs(z)))


def _early_next_tile(n_tiles, switch=1):
    return lambda i, j: (jnp.where(j < switch, i, jnp.minimum(i + 1, n_tiles - 1)), 0)


def _split3(x):
    hi = x.astype(BF16)
    r = x - hi.astype(F32)
    mid = r.astype(BF16)
    lo = (r - mid.astype(F32)).astype(BF16)
    return hi, mid, lo


def _rms_proj_kernel(x_ref, g_ref, w_ref, *rest, tn, q_cols, q_scale, with_gate):
    if with_gate:
        wf_ref, bf_ref, o_ref, logf_ref, h_ref = rest
    else:
        o_ref, h_ref = rest
    j = pl.program_id(1)

    @pl.when(j == 0)
    def _():
        h = _rmsnorm(x_ref[...], g_ref[...]).astype(BF16)
        h_ref[...] = h
        if with_gate:
            z = jnp.dot(h, wf_ref[...], preferred_element_type=F32) + bf_ref[...]
            logf_ref[...] = _log_sigmoid(z)

    acc = jnp.dot(h_ref[...], w_ref[...], preferred_element_type=F32)
    if q_cols:
        acc = acc * jnp.where(j * tn < q_cols, q_scale, 1.0).astype(F32)
    o_ref[...] = acc.astype(o_ref.dtype)


def _rms_proj(x, gain, w, layer, *, n_out, tm, tn, out_dtype, q_cols=0, q_scale=1.0, gate=None):
    T, D = x.shape
    assert T % tm == 0 and n_out % tn == 0 and q_cols % tn == 0 and n_out <= w.shape[2]
    in_specs = [
        pl.BlockSpec((tm, D), _early_next_tile(T // tm)),
        pl.BlockSpec((1, D), lambda i, j: (0, 0)),
        pl.BlockSpec((None, D, tn), lambda i, j: (layer, 0, j)),
    ]
    args = [x, gain.reshape(1, D), w]
    out_shape = [jax.ShapeDtypeStruct((T, n_out), out_dtype)]
    out_specs = [pl.BlockSpec((tm, tn), lambda i, j: (i, j))]
    if gate is not None:
        in_specs += [pl.BlockSpec((D, LANES), lambda i, j: (0, 0)),
                     pl.BlockSpec((1, LANES), lambda i, j: (0, 0))]
        args += list(gate)
        out_shape.append(jax.ShapeDtypeStruct((T, LANES), F32))
        out_specs.append(pl.BlockSpec((tm, LANES), lambda i, j: (i, 0)))
    res = pl.pallas_call(
        functools.partial(_rms_proj_kernel, tn=tn, q_cols=q_cols, q_scale=q_scale,
                          with_gate=gate is not None),
        grid=(T // tm, n_out // tn),
        in_specs=in_specs,
        out_specs=out_specs,
        out_shape=out_shape,
        scratch_shapes=[pltpu.VMEM((tm, D), BF16)],
        compiler_params=_params(),
        name="rms_proj_gate" if gate is not None else "rms_proj",
    )(*args)
    return res if gate is not None else res[0]


def _ffn_start(x_ref, g_ref, h_ref, o_ref):
    x = x_ref[...]
    h_ref[...] = _rmsnorm(x, g_ref[...]).astype(BF16)
    o_ref[...] = x


def _ffn_chunk(h_ref, base_ref, o_ref, wg, wu, wo):
    h = h_ref[...]
    gate = jnp.dot(h, wg, preferred_element_type=F32)
    up = jnp.dot(h, wu, preferred_element_type=F32)
    act = (gate * jax.nn.sigmoid(gate) * (0.5 * up)).astype(BF16)
    o_ref[...] = base_ref[...] + jnp.dot(act, wo, preferred_element_type=F32)


def _ffn_head_kernel(x_ref, g_ref, wg_ref, wu_ref, wo_ref, *rest, final_norm):
    if final_norm:
        fg_ref, o_ref, wg_out, wu_out, wo_out, h_ref = rest
    else:
        o_ref, wg_out, wu_out, wo_out, h_ref = rest
    j = pl.program_id(0)

    @pl.when(j == 0)
    def _():
        _ffn_start(x_ref, g_ref, h_ref, o_ref)

    wg, wu, wo = (w[...].astype(BF16) for w in (wg_ref, wu_ref, wo_ref))
    wg_out[...] = wg
    wu_out[...] = wu
    wo_out[...] = wo
    _ffn_chunk(h_ref, o_ref, o_ref, wg, wu, wo)

    if final_norm:
        @pl.when(j == pl.num_programs(0) - 1)
        def _():
            o_ref[...] = _rmsnorm(o_ref[...], fg_ref[...])


def _ffn_main_kernel(x_ref, g_ref, wg_ref, wu_ref, wo_ref, head_ref, *rest, final_norm, n_head):
    if final_norm:
        fg_ref, o_ref, h_ref = rest
    else:
        o_ref, h_ref = rest
    i = pl.program_id(0)
    j = pl.program_id(1)
    last = pl.num_programs(1) - 1

    @pl.when(jnp.logical_and(i < n_head, j == last))
    def _():
        o_ref[...] = head_ref[...]

    @pl.when(i >= n_head)
    def _():
        @pl.when(j == 0)
        def _():
            h_ref[...] = _rmsnorm(x_ref[...], g_ref[...]).astype(BF16)
            _ffn_chunk(h_ref, x_ref, o_ref, wg_ref[...], wu_ref[...], wo_ref[...])

        @pl.when(j > 0)
        def _():
            _ffn_chunk(h_ref, o_ref, o_ref, wg_ref[...], wu_ref[...], wo_ref[...])

        if final_norm:
            @pl.when(j == last)
            def _():
                o_ref[...] = _rmsnorm(o_ref[...], fg_ref[...])


def _ffn(x, gain, w_in, w_out, layer, *, tm, tf, head_tm, head_tf, final_gain=None):
    T, D = x.shape
    F = w_out.shape[1]
    assert T % tm == 0 and F % tf == 0 and F % head_tf == 0
    assert head_tm % tm == 0 and head_tm <= T
    final = final_gain is not None
    gain2 = gain.reshape(1, D)
    extra_specs = [pl.BlockSpec((1, D), lambda *_: (0, 0))] if final else []
    extra_args = [final_gain.reshape(1, D)] if final else []

    nfh = F // head_tf
    per = tf // head_tf
    assert tf % head_tf == 0
    once = dict(pipeline_mode=pl.Buffered(1))
    head, wg, wu, wo = pl.pallas_call(
        functools.partial(_ffn_head_kernel, final_norm=final),
        grid=(nfh,),
        in_specs=[
            pl.BlockSpec((head_tm, D), lambda j: (0, 0), **once),
            pl.BlockSpec((1, D), lambda j: (0, 0)),
            pl.BlockSpec((None, D, head_tf), lambda j: (layer, 0, j)),
            pl.BlockSpec((None, D, head_tf), lambda j: (layer, 0, j + nfh)),
            pl.BlockSpec((None, head_tf, D), lambda j: (layer, j, 0)),
        ] + extra_specs,
        out_specs=[
            pl.BlockSpec((head_tm, D), lambda j: (0, 0), **once),
            pl.BlockSpec((None, D, head_tf), lambda j: (j // per, 0, j % per)),
            pl.BlockSpec((None, D, head_tf), lambda j: (j // per, 0, j % per)),
            pl.BlockSpec((head_tf, D), lambda j: (j, 0)),
        ],
        out_shape=[
            jax.ShapeDtypeStruct((head_tm, D), F32),
            jax.ShapeDtypeStruct((F // tf, D, tf), BF16),
            jax.ShapeDtypeStruct((F // tf, D, tf), BF16),
            jax.ShapeDtypeStruct((F, D), BF16),
        ],
        scratch_shapes=[pltpu.VMEM((head_tm, D), BF16)],
        compiler_params=_params(),
        name="ffn_head_final" if final else "ffn_head",
    )(x, gain2, w_in, w_in, w_out, *extra_args)

    n_head = head_tm // tm

    def chunk(i, j):
        return jnp.where(i < n_head, 0, j)

    return pl.pallas_call(
        functools.partial(_ffn_main_kernel, final_norm=final, n_head=n_head),
        grid=(T // tm, F // tf),
        in_specs=[
            pl.BlockSpec((tm, D), lambda i, j: (i, 0)),
            pl.BlockSpec((1, D), lambda i, j: (0, 0)),
            pl.BlockSpec((None, D, tf), lambda i, j: (chunk(i, j), 0, 0)),
            pl.BlockSpec((None, D, tf), lambda i, j: (chunk(i, j), 0, 0)),
            pl.BlockSpec((tf, D), lambda i, j: (chunk(i, j), 0)),
            pl.BlockSpec((tm, D), lambda i, j: (jnp.minimum(i, n_head - 1), 0)),
        ] + extra_specs,
        out_specs=pl.BlockSpec((tm, D), lambda i, j: (i, 0)),
        out_shape=jax.ShapeDtypeStruct((T, D), F32),
        scratch_shapes=[pltpu.VMEM((tm, D), BF16)],
        compiler_params=_params(),
        name="ffn_final" if final else "ffn",
    )(x, gain2, wg, wu, wo, head, *extra_args)


def _cumsum_kernel(tri_ref, x_ref, ccol_ref, crow_ref, *, heads):
    tri = tri_ref[...]
    blk = tri.shape[0]
    carry = jnp.zeros((1, LANES), F32)
    for r0 in range(0, x_ref.shape[1], blk):
        pieces = _split3(x_ref[0, r0:r0 + blk, :] * LOG2E)
        c = carry + sum(jnp.dot(tri, p, preferred_element_type=F32) for p in pieces)
        ccol_ref[0, r0:r0 + blk, :] = c
        carry = c[blk - 1:blk, :]
    crow_ref[0] = ccol_ref[0].T[:heads]


def _fox_cumsum(logf, heads, blk=256):
    B, S, _ = logf.shape
    blk = min(blk, S)
    assert S % blk == 0
    tri = jnp.tril(jnp.ones((blk, blk), BF16))
    return pl.pallas_call(
        functools.partial(_cumsum_kernel, heads=heads),
        grid=(B,),
        in_specs=[pl.BlockSpec((blk, blk), lambda b: (0, 0)),
                  pl.BlockSpec((1, S, LANES), lambda b: (b, 0, 0))],
        out_specs=[pl.BlockSpec((1, S, LANES), lambda b: (b, 0, 0)),
                   pl.BlockSpec((1, heads, S), lambda b: (b, 0, 0))],
        out_shape=[jax.ShapeDtypeStruct((B, S, LANES), F32),
                   jax.ShapeDtypeStruct((B, heads, S), F32)],
        compiler_params=_params(),
        name="fox_cumsum",
    )(tri, logf)


def _fox_attn_kernel(q_ref, k_ref, v_ref, ccol_ref, crow_ref, o_ref, *, tile, hpb):
    hb = pl.program_id(1)
    qi = pl.program_id(2)
    lane = lax.broadcasted_iota(jnp.int32, (tile, LANES), 1)
    ccol = ccol_ref[0]
    heads = range(hpb)
    q = [q_ref[0, :, e * HEAD_DIM:(e + 1) * HEAD_DIM] for e in heads]
    cq = [jnp.sum(jnp.where(lane == hb * hpb + e, ccol, 0.0), axis=1, keepdims=True)
          for e in heads]

    def step(j, carry, diagonal):
        start = pl.multiple_of(j * tile, tile)
        out = []
        for e in heads:
            m, l, acc = carry[e]
            k = k_ref[0, pl.ds(start, tile), e * HEAD_DIM:(e + 1) * HEAD_DIM]
            v = v_ref[0, pl.ds(start, tile), e * HEAD_DIM:(e + 1) * HEAD_DIM]
            ck = crow_ref[0, e, pl.ds(j, 1), :]
            s = lax.dot_general(q[e], k, _NT, preferred_element_type=F32) - ck
            if diagonal:
                row = lax.broadcasted_iota(jnp.int32, (tile, tile), 0)
                col = lax.broadcasted_iota(jnp.int32, (tile, tile), 1)
                s = jnp.where(col <= row, s, NEG_INF)
            m_new = jnp.maximum(m, jnp.max(s, axis=1, keepdims=True) + cq[e])
            alpha = jnp.exp2(m - m_new)
            p = jnp.exp2(s - (m_new - cq[e]))
            l = alpha * l + jnp.sum(p, axis=1, keepdims=True)
            acc = alpha * acc + jnp.dot(p.astype(BF16), v, preferred_element_type=F32)
            out.append((m_new, l, acc))
        return tuple(out)

    init = tuple((jnp.full((tile, 1), NEG_INF, F32), jnp.zeros((tile, 1), F32),
                  jnp.zeros((tile, HEAD_DIM), F32)) for _ in heads)
    carry = lax.fori_loop(0, qi, lambda j, c: step(j, c, False), init)
    carry = step(qi, carry, True)
    for e in heads:
        _, l, acc = carry[e]
        o_ref[0, :, e * HEAD_DIM:(e + 1) * HEAD_DIM] = (acc / l).astype(o_ref.dtype)


def _fox_attention(qkv, ccol, crow, *, heads, tile, hpb=4):
    B, S, _ = qkv.shape
    hpb = min(hpb, heads)
    assert S % tile == 0 and heads % hpb == 0
    nq = S // tile
    nhb = heads // hpb
    w = hpb * HEAD_DIM
    crow4 = crow.reshape(B, heads, nq, tile)
    return pl.pallas_call(
        functools.partial(_fox_attn_kernel, tile=tile, hpb=hpb),
        grid=(B, nhb, nq),
        in_specs=[
            pl.BlockSpec((1, tile, w), lambda b, h, i: (b, i, h)),
            pl.BlockSpec((1, S, w), lambda b, h, i: (b, 0, nhb + h)),
            pl.BlockSpec((1, S, w), lambda b, h, i: (b, 0, 2 * nhb + h)),
            pl.BlockSpec((1, tile, LANES), lambda b, h, i: (b, i, 0)),
            pl.BlockSpec((1, hpb, nq, tile), lambda b, h, i: (b, h, 0, 0)),
        ],
        out_specs=pl.BlockSpec((1, tile, w), lambda b, h, i: (b, i, h)),
        out_shape=jax.ShapeDtypeStruct((B, S, heads * HEAD_DIM), BF16),
        compiler_params=_params(),
        name="fox_attn",
    )(qkv, qkv, qkv, ccol, crow4)


def _t5_bucket(dist):
    max_exact = NUM_BUCKETS // 2
    d = np.maximum(dist, 1).astype(np.float32)
    large = max_exact + (np.log(d / max_exact) / np.log(MAX_DISTANCE / max_exact)
                         * (NUM_BUCKETS - max_exact)).astype(np.int32)
    large = np.minimum(large, NUM_BUCKETS - 1)
    return np.where(dist < max_exact, dist, large).astype(np.int32)


def _dil_bias_tables(rel_bias, patterns, group_heads):
    period = 4 * DIL_TILE
    c = np.arange(period)
    back = c < 2 * DIL_TILE
    groups = []
    for g, (window, dil) in enumerate(patterns):
        assert window // dil == DIL_TILE
        strips = []
        for t in (0, 1):
            delta = np.where(back, t * DIL_TILE - c, t * DIL_TILE + period - c)
            valid = (delta >= 0) & (delta <= DIL_TILE) & (back | (c > period - DIL_TILE))
            bucket = _t5_bucket(np.clip(delta, 0, DIL_TILE) * dil)
            vec = rel_bias[bucket][:, g * group_heads:(g + 1) * group_heads].astype(F32)
            vec = jnp.where(valid[:, None], vec, NEG_INF).T
            flat = jnp.tile(vec, (1, DIL_TILE))[:, :DIL_TILE * (period - 1)]
            strips.append(flat.reshape(group_heads, DIL_TILE, period - 1)[:, :, :2 * DIL_TILE])
        groups.append(jnp.stack(strips, axis=1))
    return jnp.stack(groups)


def _dil_attn_kernel(*refs, dils, seq):
    n_g = len(dils)
    q_refs, k_refs, v_refs = refs[:n_g], refs[n_g:2 * n_g], refs[2 * n_g:3 * n_g]
    t_ref = refs[3 * n_g]
    o_refs = refs[3 * n_g + 1:4 * n_g + 1]
    o_sc, lse_sc = refs[4 * n_g + 1:]

    def rows(start, size, stride):
        return pl.ds(start, size) if stride == 1 else pl.ds(start, size, stride=stride)

    for g, d in enumerate(dils):
        cls_len = seq // d
        n_keys = min(2 * DIL_TILE, cls_len)
        for r in range(d):
            for u0 in range(0, cls_len, DIL_TILE):
                s0 = min(max(u0 - DIL_TILE, 0), cls_len - n_keys)
                strip = (u0 - s0) // DIL_TILE
                q_rows = rows(r + d * u0, DIL_TILE, d)
                k_rows = rows(r + d * s0, n_keys, d)
                q = q_refs[g][0, q_rows, :].astype(BF16)
                k = k_refs[g][0, k_rows, :].astype(BF16)
                v = v_refs[g][0, k_rows, :].astype(BF16)
                s = lax.dot_general(q, k, _NT, preferred_element_type=F32)
                s = s + t_ref[g, 0, strip][:, :n_keys]
                m = jnp.max(s, axis=1, keepdims=True)
                p = jnp.exp(s - m)
                l = jnp.sum(p, axis=1, keepdims=True)
                o = jnp.dot(p.astype(BF16), v, preferred_element_type=F32) / l
                o_sc[g, q_rows, :] = o
                lse_sc[g, q_rows, :] = jnp.broadcast_to(m + jnp.log(l), (DIL_TILE, LANES))

    lse = [lse_sc[g] for g in range(n_g)]
    mx = functools.reduce(jnp.maximum, lse)
    e = [jnp.exp(x - mx) for x in lse]
    inv = 1.0 / functools.reduce(lambda x, y: x + y, e)
    for g in range(n_g):
        o_refs[g][0] = (o_sc[g] * (e[g] * inv)).astype(o_refs[g].dtype)


def _dil_attention(qkv, tables, *, patterns, group_heads):
    B, S, _ = qkv.shape
    n_g = len(patterns)
    dils = tuple(d for _, d in patterns)
    assert all(S % (d * DIL_TILE) == 0 for d in dils)
    nh = n_g * group_heads

    def head_spec(part, g):
        return pl.BlockSpec((1, S, HEAD_DIM),
                            lambda b, h, part=part, g=g: (b, 0, part * nh + g * group_heads + h))

    in_specs = [head_spec(part, g) for part in range(3) for g in range(n_g)]
    in_specs.append(pl.BlockSpec((n_g, 1, 2, DIL_TILE, 2 * DIL_TILE),
                                 lambda b, h: (0, h, 0, 0, 0)))
    out_spec = pl.BlockSpec((1, S, HEAD_DIM), lambda b, h: (b, 0, h))
    return pl.pallas_call(
        functools.partial(_dil_attn_kernel, dils=dils, seq=S),
        grid=(B, group_heads),
        in_specs=in_specs,
        out_specs=[out_spec] * n_g,
        out_shape=[jax.ShapeDtypeStruct((B, S, group_heads * HEAD_DIM), BF16)] * n_g,
        scratch_shapes=[pltpu.VMEM((n_g, S, HEAD_DIM), F32), pltpu.VMEM((n_g, S, LANES), F32)],
        compiler_params=_params(),
        name="dil_attn",
    )(*([qkv] * (3 * n_g)), tables)


def _mix_cross_kernel(*refs, n_a, heads, scale):
    a_refs, wm_refs = refs[:n_a], refs[n_a:2 * n_a]
    x_ref, g_ref, wq_ref, kv_ref, wo_ref, o_ref = refs[2 * n_a:]
    x = x_ref[0]
    for a_ref, wm_ref in zip(a_refs, wm_refs):
        x = x + jnp.dot(a_ref[0], wm_ref[...], preferred_element_type=F32)
    h = _rmsnorm(x, g_ref[...]).astype(BF16)
    q = (jnp.dot(h, wq_ref[...].astype(BF16), preferred_element_type=F32) * scale).astype(BF16)
    kv = kv_ref[0]
    outs = []
    for hd in range(heads):
        qh = q[:, hd * HEAD_DIM:(hd + 1) * HEAD_DIM]
        kh = kv[:, hd * HEAD_DIM:(hd + 1) * HEAD_DIM]
        vh = kv[:, (heads + hd) * HEAD_DIM:(heads + hd + 1) * HEAD_DIM]
        s = lax.dot_general(qh, kh, _NT, preferred_element_type=F32)
        p = jnp.exp(s - jnp.max(s, axis=1, keepdims=True))
        l = jnp.sum(p, axis=1, keepdims=True)
        outs.append((jnp.dot(p.astype(BF16), vh, preferred_element_type=F32) / l).astype(BF16))
    o = jnp.concatenate(outs, axis=1)
    o_ref[0] = x + jnp.dot(o, wo_ref[...].astype(BF16), preferred_element_type=F32)


def _mix_cross(x, a_list, w_mix, mix_layer, gain, wq, kv, wo, layer, *, heads, tm):
    B, S, D = x.shape
    M = kv.shape[1]
    hd = heads * HEAD_DIM
    kg = a_list[0].shape[2]
    n_a = len(a_list)
    assert S % tm == 0 and kg * n_a == w_mix.shape[1]
    once = dict(pipeline_mode=pl.Buffered(1))
    in_specs = [pl.BlockSpec((1, tm, kg), lambda b, i: (b, i, 0)) for _ in a_list]
    in_specs += [pl.BlockSpec((None, kg, D), lambda b, i, g=g: (mix_layer, g, 0), **once)
                 for g in range(n_a)]
    in_specs += [
        pl.BlockSpec((1, tm, D), lambda b, i: (b, i, 0)),
        pl.BlockSpec((1, D), lambda b, i: (0, 0)),
        pl.BlockSpec((None, D, hd), lambda b, i: (layer, 0, 0), **once),
        pl.BlockSpec((1, M, 2 * hd), lambda b, i: (b, 0, 0)),
        pl.BlockSpec((None, hd, D), lambda b, i: (layer, 0, 0), **once),
    ]
    return pl.pallas_call(
        functools.partial(_mix_cross_kernel, n_a=n_a, heads=heads, scale=HEAD_DIM ** -0.5),
        grid=(B, S // tm),
        in_specs=in_specs,
        out_specs=pl.BlockSpec((1, tm, D), lambda b, i: (b, i, 0)),
        out_shape=jax.ShapeDtypeStruct((B, S, D), F32),
        compiler_params=_params(),
        name="mix_cross",
    )(*a_list, *([w_mix] * n_a), x, gain.reshape(1, D), wq, kv, wo)


def _fox_mixer(x, gain, w_in, w_gate, b_f, layer, *, heads, attn_tile, tm):
    B, S, D = x.shape
    hd = heads * HEAD_DIM
    w_f = jnp.pad(w_gate, ((0, 0), (0, LANES - heads))).astype(BF16)
    b_pad = jnp.pad(b_f, (0, LANES - heads)).reshape(1, LANES).astype(F32)
    qkv, logf = _rms_proj(x.reshape(B * S, D), gain, w_in, layer, n_out=3 * hd, tm=tm,
                          tn=min(1024, hd), out_dtype=BF16, q_cols=hd,
                          q_scale=HEAD_DIM ** -0.5 * LOG2E, gate=(w_f, b_pad))
    ccol, crow = _fox_cumsum(logf.reshape(B, S, LANES), heads)
    return [_fox_attention(qkv.reshape(B, S, 3 * hd), ccol, crow, heads=heads, tile=attn_tile)]


def _dilated_mixer(x, gain, w_in, layer, rel_bias, *, patterns, group_heads, tm):
    B, S, D = x.shape
    n_g = len(patterns)
    hd = n_g * group_heads * HEAD_DIM
    qkv = _rms_proj(x.reshape(B * S, D), gain, w_in, layer, n_out=3 * hd, tm=tm,
                    tn=group_heads * HEAD_DIM, out_dtype=F32, q_cols=hd, q_scale=HEAD_DIM ** -0.5)
    tables = _dil_bias_tables(rel_bias, patterns, group_heads)
    return _dil_attention(qkv.reshape(B, S, 3 * hd), tables, patterns=patterns,
                          group_heads=group_heads)


def _forward(x, mem, ffn1_norm, ffn1_w_in, ffn1_w_out, mix_norm, fox_w_in, fox_b_f, fox_w_out,
             dil_w_in, dil_w_out, rel_bias, cross_norm, mem_norm, cross_w_q, cross_w_kv,
             cross_w_out, ffn2_norm, ffn2_w_in, ffn2_w_out, final_norm, *,
             patterns, fox_heads, group_heads, cross_heads, tm, tf, head_tm, head_tf, proj_tm,
             attn_tile, cross_tm):
    B, S, D = x.shape
    M = mem.shape[1]
    depth = ffn1_norm.shape[0]
    n_mixers = 2
    fox_hd = fox_heads * HEAD_DIM
    cross_hd = cross_heads * HEAD_DIM
    fox_w_gate = fox_w_in[:, :, 3 * fox_hd:]
    fox_w_in, fox_w_out, dil_w_in, dil_w_out, cross_w_kv = (
        w.astype(BF16) for w in (fox_w_in, fox_w_out, dil_w_in, dil_w_out, cross_w_kv))
    ffn_tiles = dict(tm=tm, tf=tf, head_tm=head_tm, head_tf=head_tf)
    for i in range(depth):
        x = _ffn(x.reshape(B * S, D), ffn1_norm[i], ffn1_w_in, ffn1_w_out, i,
                 **ffn_tiles).reshape(B, S, D)
        j = i // n_mixers
        if i % n_mixers == 0:
            mixed = _fox_mixer(x, mix_norm[i], fox_w_in, fox_w_gate[j], fox_b_f[j], j,
                               heads=fox_heads, attn_tile=attn_tile, tm=proj_tm)
            w_mix = fox_w_out
        else:
            mixed = _dilated_mixer(x, mix_norm[i], dil_w_in, j, rel_bias,
                                   patterns=patterns, group_heads=group_heads, tm=proj_tm)
            w_mix = dil_w_out
        kv = _rms_proj(mem.reshape(B * M, D), mem_norm, cross_w_kv, i, n_out=2 * cross_hd,
                       tm=min(tm, B * M), tn=min(512, cross_hd), out_dtype=BF16)
        x = _mix_cross(x, mixed, w_mix, j, cross_norm[i], cross_w_q,
                       kv.reshape(B, M, 2 * cross_hd), cross_w_out, i,
                       heads=cross_heads, tm=cross_tm)
        x = _ffn(x.reshape(B * S, D), ffn2_norm[i], ffn2_w_in, ffn2_w_out, i, **ffn_tiles,
                 final_gain=final_norm if i == depth - 1 else None).reshape(B, S, D)
    return x


def kernel(x, mem, ffn1_norm, ffn1_w_in, ffn1_w_out, mix_norm, fox_w_in, fox_b_f, fox_w_out, dil_w_in, dil_w_out, rel_bias, cross_norm, mem_norm, cross_w_q, cross_w_kv, cross_w_out, ffn2_norm, ffn2_w_in, ffn2_w_out, final_norm):
    return _forward(x, mem, ffn1_norm, ffn1_w_in, ffn1_w_out, mix_norm, fox_w_in, fox_b_f,
                    fox_w_out, dil_w_in, dil_w_out, rel_bias, cross_norm, mem_norm, cross_w_q,
                    cross_w_kv, cross_w_out, ffn2_norm, ffn2_w_in, ffn2_w_out, final_norm,
                    patterns=DIL_PATTERNS, fox_heads=FOX_HEADS, group_heads=DIL_GROUP_HEADS,
                    cross_heads=CROSS_HEADS, tm=512, tf=512, head_tm=1024, head_tf=256,
                    proj_tm=1024, attn_tile=512, cross_tm=512)
```

```python
import functools

import numpy as np
import jax
import jax.numpy as jnp
from jax import lax
from jax.experimental import pallas as pl
from jax.experimental.pallas import tpu as pltpu

HEAD_DIM = 128
FOX_HEADS = 16
DIL_PATTERNS = ((128, 1), (512, 4), (2048, 16))
DIL_GROUP_HEADS = 6
CROSS_HEADS = 4
NUM_BUCKETS = 32
MAX_DISTANCE = 2048
RMS_EPS = 1e-6
NEG_INF = -1e30

LANES = 128
DIL_TILE = 128
VMEM_LIMIT_BYTES = 56 * 1024 * 1024

BF16 = jnp.bfloat16
F32 = jnp.float32
_NT = (((1,), (1,)), ((), ()))
LOG2E = 1.4426950408889634


def _params():
    return pltpu.CompilerParams(vmem_limit_bytes=VMEM_LIMIT_BYTES)


def _rmsnorm(x, g):
    return x * lax.rsqrt(jnp.mean(x * x, axis=-1, keepdims=True) + RMS_EPS) * g


def _log_sigmoid(z):
    return jnp.minimum(z, 0.0) - jnp.log1p(jnp.exp(-jnp.abs(z)))


def _early_next_tile(n_tiles, switch=1):
    return lambda i, j: (jnp.where(j < switch, i, jnp.minimum(i + 1, n_tiles - 1)), 0)


def _split3(x):
    hi = x.astype(BF16)
    r = x - hi.astype(F32)
    mid = r.astype(BF16)
    lo = (r - mid.astype(F32)).astype(BF16)
    return hi, mid, lo


def _rms_proj_kernel(x_ref, g_ref, w_ref, *rest, tn, q_cols, q_scale, with_gate):
    if with_gate:
        wf_ref, bf_ref, o_ref, logf_ref, h_ref = rest
    else:
        o_ref, h_ref = rest
    j = pl.program_id(1)

    def project(h):
        acc = jnp.dot(h, w_ref[...], preferred_element_type=F32)
        if q_cols:
            acc = acc * jnp.where(j * tn < q_cols, q_scale, 1.0).astype(F32)
        o_ref[...] = acc.astype(o_ref.dtype)

    @pl.when(j == 0)
    def _():
        h = _rmsnorm(x_ref[...], g_ref[...]).astype(BF16)
        h_ref[...] = h
        if with_gate:
            z = jnp.dot(h, wf_ref[...], preferred_element_type=F32) + bf_ref[...]
            logf_ref[...] = _log_sigmoid(z)
        project(h)

    @pl.when(j > 0)
    def _():
        project(h_ref[...])


def _rms_proj(x, gain, w, layer, *, n_out, tm, tn, out_dtype, q_cols=0, q_scale=1.0, gate=None):
    T, D = x.shape
    assert T % tm == 0 and n_out % tn == 0 and q_cols % tn == 0 and n_out <= w.shape[2]
    in_specs = [
        pl.BlockSpec((tm, D), _early_next_tile(T // tm)),
        pl.BlockSpec((1, D), lambda i, j: (0, 0)),
        pl.BlockSpec((None, D, tn), lambda i, j: (layer, 0, j)),
    ]
    args = [x, gain.reshape(1, D), w]
    out_shape = [jax.ShapeDtypeStruct((T, n_out), out_dtype)]
    out_specs = [pl.BlockSpec((tm, tn), lambda i, j: (i, j))]
    if gate is not None:
        in_specs += [pl.BlockSpec((D, LANES), lambda i, j: (0, 0)),
                     pl.BlockSpec((1, LANES), lambda i, j: (0, 0))]
        args += list(gate)
        out_shape.append(jax.ShapeDtypeStruct((T, LANES), F32))
        out_specs.append(pl.BlockSpec((tm, LANES), lambda i, j: (i, 0)))
    res = pl.pallas_call(
        functools.partial(_rms_proj_kernel, tn=tn, q_cols=q_cols, q_scale=q_scale,
                          with_gate=gate is not None),
        grid=(T // tm, n_out // tn),
        in_specs=in_specs,
        out_specs=out_specs,
        out_shape=out_shape,
        scratch_shapes=[pltpu.VMEM((tm, D), BF16)],
        compiler_params=_params(),
        name="rms_proj_gate" if gate is not None else "rms_proj",
    )(*args)
    return res if gate is not None else res[0]


def _ffn_start(x_ref, g_ref, h_ref, o_ref):
    x = x_ref[...]
    h_ref[...] = _rmsnorm(x, g_ref[...]).astype(BF16)
    o_ref[...] = x


def _ffn_chunk(h_ref, base_ref, o_ref, wg, wu, wo):
    h = h_ref[...]
    gate = jnp.dot(h, wg, preferred_element_type=F32)
    up = jnp.dot(h, wu, preferred_element_type=F32)
    act = (gate * jax.nn.sigmoid(gate) * (0.5 * up)).astype(BF16)
    o_ref[...] = base_ref[...] + jnp.dot(act, wo, preferred_element_type=F32)


def _ffn_head_kernel(x_ref, g_ref, wg_ref, wu_ref, wo_ref, *rest, final_norm):
    if final_norm:
        fg_ref, o_ref, wg_out, wu_out, wo_out, h_ref = rest
    else:
        o_ref, wg_out, wu_out, wo_out, h_ref = rest
    j = pl.program_id(0)

    @pl.when(j == 0)
    def _():
        _ffn_start(x_ref, g_ref, h_ref, o_ref)

    wg, wu, wo = (w[...].astype(BF16) for w in (wg_ref, wu_ref, wo_ref))
    wg_out[...] = wg
    wu_out[...] = wu
    wo_out[...] = wo
    _ffn_chunk(h_ref, o_ref, o_ref, wg, wu, wo)

    if final_norm:
        @pl.when(j == pl.num_programs(0) - 1)
        def _():
            o_ref[...] = _rmsnorm(o_ref[...], fg_ref[...])


def _ffn_main_kernel(x_ref, g_ref, wg_ref, wu_ref, wo_ref, head_ref, *rest, final_norm, n_head):
    if final_norm:
        fg_ref, o_ref, h_ref = rest
    else:
        o_ref, h_ref = rest
    i = pl.program_id(0)
    j = pl.program_id(1)
    last = pl.num_programs(1) - 1

    @pl.when(jnp.logical_and(i < n_head, j == last))
    def _():
        o_ref[...] = head_ref[...]

    @pl.when(i >= n_head)
    def _():
        @pl.when(j == 0)
        def _():
            h_ref[...] = _rmsnorm(x_ref[...], g_ref[...]).astype(BF16)
            _ffn_chunk(h_ref, x_ref, o_ref, wg_ref[...], wu_ref[...], wo_ref[...])

        @pl.when(j > 0)
        def _():
            _ffn_chunk(h_ref, o_ref, o_ref, wg_ref[...], wu_ref[...], wo_ref[...])

        if final_norm:
            @pl.when(j == last)
            def _():
                o_ref[...] = _rmsnorm(o_ref[...], fg_ref[...])


def _ffn(x, gain, w_in, w_out, layer, *, tm, tf, head_tm, head_tf, final_gain=None):
    T, D = x.shape
    F = w_out.shape[1]
    assert T % tm == 0 and F % tf == 0 and F % head_tf == 0
    assert head_tm % tm == 0 and head_tm <= T
    final = final_gain is not None
    gain2 = gain.reshape(1, D)
    extra_specs = [pl.BlockSpec((1, D), lambda *_: (0, 0))] if final else []
    extra_args = [final_gain.reshape(1, D)] if final else []

    nfh = F // head_tf
    per = tf // head_tf
    assert tf % head_tf == 0
    once = dict(pipeline_mode=pl.Buffered(1))
    head, wg, wu, wo = pl.pallas_call(
        functools.partial(_ffn_head_kernel, final_norm=final),
        grid=(nfh,),
        in_specs=[
            pl.BlockSpec((head_tm, D), lambda j: (0, 0), **once),
            pl.BlockSpec((1, D), lambda j: (0, 0)),
            pl.BlockSpec((None, D, head_tf), lambda j: (layer, 0, j)),
            pl.BlockSpec((None, D, head_tf), lambda j: (layer, 0, j + nfh)),
            pl.BlockSpec((None, head_tf, D), lambda j: (layer, j, 0)),
        ] + extra_specs,
        out_specs=[
            pl.BlockSpec((head_tm, D), lambda j: (0, 0), **once),
            pl.BlockSpec((None, D, head_tf), lambda j: (j // per, 0, j % per)),
            pl.BlockSpec((None, D, head_tf), lambda j: (j // per, 0, j % per)),
            pl.BlockSpec((head_tf, D), lambda j: (j, 0)),
        ],
        out_shape=[
            jax.ShapeDtypeStruct((head_tm, D), F32),
            jax.ShapeDtypeStruct((F // tf, D, tf), BF16),
            jax.ShapeDtypeStruct((F // tf, D, tf), BF16),
            jax.ShapeDtypeStruct((F, D), BF16),
        ],
        scratch_shapes=[pltpu.VMEM((head_tm, D), BF16)],
        compiler_params=_params(),
        name="ffn_head_final" if final else "ffn_head",
    )(x, gain2, w_in, w_in, w_out, *extra_args)

    n_head = head_tm // tm

    def chunk(i, j):
        return jnp.where(i < n_head, 0, j)

    return pl.pallas_call(
        functools.partial(_ffn_main_kernel, final_norm=final, n_head=n_head),
        grid=(T // tm, F // tf),
        in_specs=[
            pl.BlockSpec((tm, D), lambda i, j: (i, 0)),
            pl.BlockSpec((1, D), lambda i, j: (0, 0)),
            pl.BlockSpec((None, D, tf), lambda i, j: (chunk(i, j), 0, 0)),
            pl.BlockSpec((None, D, tf), lambda i, j: (chunk(i, j), 0, 0)),
            pl.BlockSpec((tf, D), lambda i, j: (chunk(i, j), 0)),
            pl.BlockSpec((tm, D), lambda i, j: (jnp.minimum(i, n_head - 1), 0)),
        ] + extra_specs,
        out_specs=pl.BlockSpec((tm, D), lambda i, j: (i, 0)),
        out_shape=jax.ShapeDtypeStruct((T, D), F32),
        scratch_shapes=[pltpu.VMEM((tm, D), BF16)],
        compiler_params=_params(),
        name="ffn_final" if final else "ffn",
    )(x, gain2, wg, wu, wo, head, *extra_args)


def _cumsum_kernel(tri_ref, x_ref, ccol_ref, crow_ref, *, heads):
    tri = tri_ref[...]
    blk = tri.shape[0]
    carry = jnp.zeros((1, LANES), F32)
    for r0 in range(0, x_ref.shape[1], blk):
        pieces = _split3(x_ref[0, r0:r0 + blk, :] * LOG2E)
        c = carry + sum(jnp.dot(tri, p, preferred_element_type=F32) for p in pieces)
        ccol_ref[0, r0:r0 + blk, :] = c
        carry = c[blk - 1:blk, :]
    crow_ref[0] = ccol_ref[0].T[:heads]


def _fox_cumsum(logf, heads, blk=256):
    B, S, _ = logf.shape
    blk = min(blk, S)
    assert S % blk == 0
    tri = jnp.tril(jnp.ones((blk, blk), BF16))
    return pl.pallas_call(
        functools.partial(_cumsum_kernel, heads=heads),
        grid=(B,),
        in_specs=[pl.BlockSpec((blk, blk), lambda b: (0, 0)),
                  pl.BlockSpec((1, S, LANES), lambda b: (b, 0, 0))],
        out_specs=[pl.BlockSpec((1, S, LANES), lambda b: (b, 0, 0)),
                   pl.BlockSpec((1, heads, S), lambda b: (b, 0, 0))],
        out_shape=[jax.ShapeDtypeStruct((B, S, LANES), F32),
                   jax.ShapeDtypeStruct((B, heads, S), F32)],
        compiler_params=_params(),
        name="fox_cumsum",
    )(tri, logf)


def _fox_attn_kernel(q_ref, k_ref, v_ref, ccol_ref, crow_ref, o_ref, *, tile, hpb):
    hb = pl.program_id(1)
    qi = pl.program_id(2)
    lane = lax.broadcasted_iota(jnp.int32, (tile, LANES), 1)
    ccol = ccol_ref[0]
    heads = range(hpb)
    q = [q_ref[0, :, e * HEAD_DIM:(e + 1) * HEAD_DIM] for e in heads]
    cq = [jnp.sum(jnp.where(lane == hb * hpb + e, ccol, 0.0), axis=1, keepdims=True)
          for e in heads]

    def step(j, carry, diagonal):
        start = pl.multiple_of(j * tile, tile)
        out = []
        for e in heads:
            m, l, acc = carry[e]
            k = k_ref[0, pl.ds(start, tile), e * HEAD_DIM:(e + 1) * HEAD_DIM]
            v = v_ref[0, pl.ds(start, tile), e * HEAD_DIM:(e + 1) * HEAD_DIM]
            ck = crow_ref[0, e, pl.ds(j, 1), :]
            s = lax.dot_general(q[e], k, _NT, preferred_element_type=F32) - ck
            if diagonal:
                row = lax.broadcasted_iota(jnp.int32, (tile, tile), 0)
                col = lax.broadcasted_iota(jnp.int32, (tile, tile), 1)
                s = jnp.where(col <= row, s, NEG_INF)
            m_new = jnp.maximum(m, jnp.max(s, axis=1, keepdims=True) + cq[e])
            alpha = jnp.exp2(m - m_new)
            p = jnp.exp2(s - (m_new - cq[e]))
            l = alpha * l + jnp.sum(p, axis=1, keepdims=True)
            acc = alpha * acc + jnp.dot(p.astype(BF16), v, preferred_element_type=F32)
            out.append((m_new, l, acc))
        return tuple(out)

    init = tuple((jnp.full((tile, 1), NEG_INF, F32), jnp.zeros((tile, 1), F32),
                  jnp.zeros((tile, HEAD_DIM), F32)) for _ in heads)
    carry = lax.fori_loop(0, qi, lambda j, c: step(j, c, False), init)
    carry = step(qi, carry, True)
    for e in heads:
        _, l, acc = carry[e]
        o_ref[0, :, e * HEAD_DIM:(e + 1) * HEAD_DIM] = (acc / l).astype(o_ref.dtype)


def _fox_attention(qkv, ccol, crow, *, heads, tile, hpb=4):
    B, S, _ = qkv.shape
    hpb = min(hpb, heads)
    assert S % tile == 0 and heads % hpb == 0
    nq = S // tile
    nhb = heads // hpb
    w = hpb * HEAD_DIM
    crow4 = crow.reshape(B, heads, nq, tile)
    return pl.pallas_call(
        functools.partial(_fox_attn_kernel, tile=tile, hpb=hpb),
        grid=(B, nhb, nq),
        in_specs=[
            pl.BlockSpec((1, tile, w), lambda b, h, i: (b, i, h)),
            pl.BlockSpec((1, S, w), lambda b, h, i: (b, 0, nhb + h)),
            pl.BlockSpec((1, S, w), lambda b, h, i: (b, 0, 2 * nhb + h)),
            pl.BlockSpec((1, tile, LANES), lambda b, h, i: (b, i, 0)),
            pl.BlockSpec((1, hpb, nq, tile), lambda b, h, i: (b, h, 0, 0)),
        ],
        out_specs=pl.BlockSpec((1, tile, w), lambda b, h, i: (b, i, h)),
        out_shape=jax.ShapeDtypeStruct((B, S, heads * HEAD_DIM), BF16),
        compiler_params=_params(),
        name="fox_attn",
    )(qkv, qkv, qkv, ccol, crow4)


def _t5_bucket(dist):
    max_exact = NUM_BUCKETS // 2
    d = np.maximum(dist, 1).astype(np.float32)
    large = max_exact + (np.log(d / max_exact) / np.log(MAX_DISTANCE / max_exact)
                         * (NUM_BUCKETS - max_exact)).astype(np.int32)
    large = np.minimum(large, NUM_BUCKETS - 1)
    return np.where(dist < max_exact, dist, large).astype(np.int32)


def _dil_bias_tables(rel_bias, patterns, group_heads):
    period = 4 * DIL_TILE
    c = np.arange(period)
    back = c < 2 * DIL_TILE
    groups = []
    for g, (window, dil) in enumerate(patterns):
        assert window // dil == DIL_TILE
        strips = []
        for t in (0, 1):
            delta = np.where(back, t * DIL_TILE - c, t * DIL_TILE + period - c)
            valid = (delta >= 0) & (delta <= DIL_TILE) & (back | (c > period - DIL_TILE))
            bucket = _t5_bucket(np.clip(delta, 0, DIL_TILE) * dil)
            vec = rel_bias[bucket][:, g * group_heads:(g + 1) * group_heads].astype(F32)
            vec = jnp.where(valid[:, None], vec, NEG_INF).T
            flat = jnp.tile(vec, (1, DIL_TILE))[:, :DIL_TILE * (period - 1)]
            strips.append(flat.reshape(group_heads, DIL_TILE, period - 1)[:, :, :2 * DIL_TILE])
        groups.append(jnp.stack(strips, axis=1))
    return jnp.stack(groups)


def _dil_attn_kernel(*refs, dils, seq):
    n_g = len(dils)
    q_refs, k_refs, v_refs = refs[:n_g], refs[n_g:2 * n_g], refs[2 * n_g:3 * n_g]
    t_ref = refs[3 * n_g]
    o_refs = refs[3 * n_g + 1:4 * n_g + 1]
    o_sc, lse_sc = refs[4 * n_g + 1:]

    def rows(start, size, stride):
        return pl.ds(start, size) if stride == 1 else pl.ds(start, size, stride=stride)

    for g, d in enumerate(dils):
        cls_len = seq // d
        n_keys = min(2 * DIL_TILE, cls_len)
        for r in range(d):
            for u0 in range(0, cls_len, DIL_TILE):
                s0 = min(max(u0 - DIL_TILE, 0), cls_len - n_keys)
                strip = (u0 - s0) // DIL_TILE
                q_rows = rows(r + d * u0, DIL_TILE, d)
                k_rows = rows(r + d * s0, n_keys, d)
                q = q_refs[g][0, q_rows, :].astype(BF16)
                k = k_refs[g][0, k_rows, :].astype(BF16)
                v = v_refs[g][0, k_rows, :].astype(BF16)
                s = lax.dot_general(q, k, _NT, preferred_element_type=F32)
                s = s + t_ref[g, 0, strip][:, :n_keys]
                m = jnp.max(s, axis=1, keepdims=True)
                p = jnp.exp(s - m)
                l = jnp.sum(p, axis=1, keepdims=True)
                o = jnp.dot(p.astype(BF16), v, preferred_element_type=F32) / l
                o_sc[g, q_rows, :] = o
                lse_sc[g, q_rows, :] = jnp.broadcast_to(m + jnp.log(l), (DIL_TILE, LANES))

    lse = [lse_sc[g] for g in range(n_g)]
    mx = functools.reduce(jnp.maximum, lse)
    e = [jnp.exp(x - mx) for x in lse]
    inv = 1.0 / functools.reduce(lambda x, y: x + y, e)
    for g in range(n_g):
        o_refs[g][0] = (o_sc[g] * (e[g] * inv)).astype(o_refs[g].dtype)


def _dil_attention(qkv, tables, *, patterns, group_heads):
    B, S, _ = qkv.shape
    n_g = len(patterns)
    dils = tuple(d for _, d in patterns)
    assert all(S % (d * DIL_TILE) == 0 for d in dils)
    nh = n_g * group_heads

    def head_spec(part, g):
        return pl.BlockSpec((1, S, HEAD_DIM),
                            lambda b, h, part=part, g=g: (b, 0, part * nh + g * group_heads + h))

    in_specs = [head_spec(part, g) for part in range(3) for g in range(n_g)]
    in_specs.append(pl.BlockSpec((n_g, 1, 2, DIL_TILE, 2 * DIL_TILE),
                                 lambda b, h: (0, h, 0, 0, 0)))
    out_spec = pl.BlockSpec((1, S, HEAD_DIM), lambda b, h: (b, 0, h))
    return pl.pallas_call(
        functools.partial(_dil_attn_kernel, dils=dils, seq=S),
        grid=(B, group_heads),
        in_specs=in_specs,
        out_specs=[out_spec] * n_g,
        out_shape=[jax.ShapeDtypeStruct((B, S, group_heads * HEAD_DIM), BF16)] * n_g,
        scratch_shapes=[pltpu.VMEM((n_g, S, HEAD_DIM), F32), pltpu.VMEM((n_g, S, LANES), F32)],
        compiler_params=_params(),
        name="dil_attn",
    )(*([qkv] * (3 * n_g)), tables)


def _mix_cross_kernel(*refs, n_a, heads, scale):
    a_refs, wm_refs = refs[:n_a], refs[n_a:2 * n_a]
    x_ref, g_ref, wq_ref, kv_ref, wo_ref, o_ref = refs[2 * n_a:]
    x = x_ref[0]
    for a_ref, wm_ref in zip(a_refs, wm_refs):
        x = x + jnp.dot(a_ref[0], wm_ref[...], preferred_element_type=F32)
    h = _rmsnorm(x, g_ref[...]).astype(BF16)
    q = (jnp.dot(h, wq_ref[...].astype(BF16), preferred_element_type=F32) * scale).astype(BF16)
    kv = kv_ref[0]
    outs = []
    for hd in range(heads):
        qh = q[:, hd * HEAD_DIM:(hd + 1) * HEAD_DIM]
        kh = kv[:, hd * HEAD_DIM:(hd + 1) * HEAD_DIM]
        vh = kv[:, (heads + hd) * HEAD_DIM:(heads + hd + 1) * HEAD_DIM]
        s = lax.dot_general(qh, kh, _NT, preferred_element_type=F32)
        p = jnp.exp(s - jnp.max(s, axis=1, keepdims=True))
        l = jnp.sum(p, axis=1, keepdims=True)
        outs.append((jnp.dot(p.astype(BF16), vh, preferred_element_type=F32) / l).astype(BF16))
    o = jnp.concatenate(outs, axis=1)
    o_ref[0] = x + jnp.dot(o, wo_ref[...].astype(BF16), preferred_element_type=F32)


def _mix_cross(x, a_list, w_mix, mix_layer, gain, wq, kv, wo, layer, *, heads, tm):
    B, S, D = x.shape
    M = kv.shape[1]
    hd = heads * HEAD_DIM
    kg = a_list[0].shape[2]
    n_a = len(a_list)
    assert S % tm == 0 and kg * n_a == w_mix.shape[1]
    once = dict(pipeline_mode=pl.Buffered(1))
    in_specs = [pl.BlockSpec((1, tm, kg), lambda b, i: (b, i, 0)) for _ in a_list]
    in_specs += [pl.BlockSpec((None, kg, D), lambda b, i, g=g: (mix_layer, g, 0), **once)
                 for g in range(n_a)]
    in_specs += [
        pl.BlockSpec((1, tm, D), lambda b, i: (b, i, 0)),
        pl.BlockSpec((1, D), lambda b, i: (0, 0)),
        pl.BlockSpec((None, D, hd), lambda b, i: (layer, 0, 0), **once),
        pl.BlockSpec((1, M, 2 * hd), lambda b, i: (b, 0, 0)),
        pl.BlockSpec((None, hd, D), lambda b, i: (layer, 0, 0), **once),
    ]
    return pl.pallas_call(
        functools.partial(_mix_cross_kernel, n_a=n_a, heads=heads, scale=HEAD_DIM ** -0.5),
        grid=(B, S // tm),
        in_specs=in_specs,
        out_specs=pl.BlockSpec((1, tm, D), lambda b, i: (b, i, 0)),
        out_shape=jax.ShapeDtypeStruct((B, S, D), F32),
        compiler_params=_params(),
        name="mix_cross",
    )(*a_list, *([w_mix] * n_a), x, gain.reshape(1, D), wq, kv, wo)


def _fox_mixer(x, gain, w_in, w_gate, b_f, layer, *, heads, attn_tile, tm):
    B, S, D = x.shape
    hd = heads * HEAD_DIM
    w_f = jnp.pad(w_gate, ((0, 0), (0, LANES - heads))).astype(BF16)
    b_pad = jnp.pad(b_f, (0, LANES - heads)).reshape(1, LANES).astype(F32)
    qkv, logf = _rms_proj(x.reshape(B * S, D), gain, w_in, layer, n_out=3 * hd, tm=tm,
                          tn=min(1024, hd), out_dtype=BF16, q_cols=hd,
                          q_scale=HEAD_DIM ** -0.5 * LOG2E, gate=(w_f, b_pad))
    ccol, crow = _fox_cumsum(logf.reshape(B, S, LANES), heads)
    return [_fox_attention(qkv.reshape(B, S, 3 * hd), ccol, crow, heads=heads, tile=attn_tile)]


def _dilated_mixer(x, gain, w_in, layer, rel_bias, *, patterns, group_heads, tm):
    B, S, D = x.shape
    n_g = len(patterns)
    hd = n_g * group_heads * HEAD_DIM
    qkv = _rms_proj(x.reshape(B * S, D), gain, w_in, layer, n_out=3 * hd, tm=tm,
                    tn=group_heads * HEAD_DIM, out_dtype=F32, q_cols=hd, q_scale=HEAD_DIM ** -0.5)
    tables = _dil_bias_tables(rel_bias, patterns, group_heads)
    return _dil_attention(qkv.reshape(B, S, 3 * hd), tables, patterns=patterns,
                          group_heads=group_heads)


def _forward(x, mem, ffn1_norm, ffn1_w_in, ffn1_w_out, mix_norm, fox_w_in, fox_b_f, fox_w_out,
             dil_w_in, dil_w_out, rel_bias, cross_norm, mem_norm, cross_w_q, cross_w_kv,
             cross_w_out, ffn2_norm, ffn2_w_in, ffn2_w_out, final_norm, *,
             patterns, fox_heads, group_heads, cross_heads, tm, tf, head_tm, head_tf, proj_tm,
             attn_tile, cross_tm):
    B, S, D = x.shape
    M = mem.shape[1]
    depth = ffn1_norm.shape[0]
    n_mixers = 2
    fox_hd = fox_heads * HEAD_DIM
    cross_hd = cross_heads * HEAD_DIM
    fox_w_gate = fox_w_in[:, :, 3 * fox_hd:]
    fox_w_in, fox_w_out, dil_w_in, dil_w_out, cross_w_kv = (
        w.astype(BF16) for w in (fox_w_in, fox_w_out, dil_w_in, dil_w_out, cross_w_kv))
    ffn_tiles = dict(tm=tm, tf=tf, head_tm=head_tm, head_tf=head_tf)
    for i in range(depth):
        x = _ffn(x.reshape(B * S, D), ffn1_norm[i], ffn1_w_in, ffn1_w_out, i,
                 **ffn_tiles).reshape(B, S, D)
        j = i // n_mixers
        if i % n_mixers == 0:
            mixed = _fox_mixer(x, mix_norm[i], fox_w_in, fox_w_gate[j], fox_b_f[j], j,
                               heads=fox_heads, attn_tile=attn_tile, tm=proj_tm)
            w_mix = fox_w_out
        else:
            mixed = _dilated_mixer(x, mix_norm[i], dil_w_in, j, rel_bias,
                                   patterns=patterns, group_heads=group_heads, tm=proj_tm)
            w_mix = dil_w_out
        kv = _rms_proj(mem.reshape(B * M, D), mem_norm, cross_w_kv, i, n_out=2 * cross_hd,
                       tm=min(tm, B * M), tn=min(512, cross_hd), out_dtype=BF16)
        x = _mix_cross(x, mixed, w_mix, j, cross_norm[i], cross_w_q,
                       kv.reshape(B, M, 2 * cross_hd), cross_w_out, i,
                       heads=cross_heads, tm=cross_tm)
        x = _ffn(x.reshape(B * S, D), ffn2_norm[i], ffn2_w_in, ffn2_w_out, i, **ffn_tiles,
                 final_gain=final_norm if i == depth - 1 else None).reshape(B, S, D)
    return x


def kernel(x, mem, ffn1_norm, ffn1_w_in, ffn1_w_out, mix_norm, fox_w_in, fox_b_f, fox_w_out, dil_w_in, dil_w_out, rel_bias, cross_norm, mem_norm, cross_w_q, cross_w_kv, cross_w_out, ffn2_norm, ffn2_w_in, ffn2_w_out, final_norm):
    return _forward(x, mem, ffn1_norm, ffn1_w_in, ffn1_w_out, mix_norm, fox_w_in, fox_b_f,
                    fox_w_out, dil_w_in, dil_w_out, rel_bias, cross_norm, mem_norm, cross_w_q,
                    cross_w_kv, cross_w_out, ffn2_norm, ffn2_w_in, ffn2_w_out, final_norm,
                    patterns=DIL_PATTERNS, fox_heads=FOX_HEADS, group_heads=DIL_GROUP_HEADS,
                    cross_heads=CROSS_HEADS, tm=512, tf=512, head_tm=1024, head_tf=256,
                    proj_tm=1024, attn_tile=512, cross_tm=512)
```

```python
import functools

import numpy as np
import jax
import jax.numpy as jnp
from jax import lax
from jax.experimental import pallas as pl
from jax.experimental.pallas import tpu as pltpu

HEAD_DIM = 128
FOX_HEADS = 16
DIL_PATTERNS = ((128, 1), (512, 4), (2048, 16))
DIL_GROUP_HEADS = 6
CROSS_HEADS = 4
NUM_BUCKETS = 32
MAX_DISTANCE = 2048
RMS_EPS = 1e-6
NEG_INF = -1e30

LANES = 128
DIL_TILE = 128
VMEM_LIMIT_BYTES = 56 * 1024 * 1024

BF16 = jnp.bfloat16
F32 = jnp.float32
_NT = (((1,), (1,)), ((), ()))
LOG2E = 1.4426950408889634


def _params():
    return pltpu.CompilerParams(vmem_limit_bytes=VMEM_LIMIT_BYTES)


def _rmsnorm(x, g):
    return x * lax.rsqrt(jnp.mean(x * x, axis=-1, keepdims=True) + RMS_EPS) * g


def _log_sigmoid(z):
    return jnp.minimum(z, 0.0) - jnp.log1p(jnp.exp(-jnp.abs(z)))


def _early_next_tile(n_tiles):
    return lambda i, j: (jnp.where(j == 0, i, jnp.minimum(i + 1, n_tiles - 1)), 0)


def _split3(x):
    hi = x.astype(BF16)
    r = x - hi.astype(F32)
    mid = r.astype(BF16)
    lo = (r - mid.astype(F32)).astype(BF16)
    return hi, mid, lo


def _rms_proj_kernel(x_ref, g_ref, w_ref, *rest, tn, q_cols, q_scale, with_gate):
    if with_gate:
        wf_ref, bf_ref, o_ref, logf_ref, h_ref = rest
    else:
        o_ref, h_ref = rest
    j = pl.program_id(1)

    @pl.when(j == 0)
    def _():
        h = _rmsnorm(x_ref[...], g_ref[...]).astype(BF16)
        h_ref[...] = h
        if with_gate:
            z = jnp.dot(h, wf_ref[...], preferred_element_type=F32) + bf_ref[...]
            logf_ref[...] = _log_sigmoid(z)

    acc = jnp.dot(h_ref[...], w_ref[...], preferred_element_type=F32)
    if q_cols:
        acc = acc * jnp.where(j * tn < q_cols, q_scale, 1.0).astype(F32)
    o_ref[...] = acc.astype(o_ref.dtype)


def _rms_proj(x, gain, w, layer, *, n_out, tm, tn, out_dtype, q_cols=0, q_scale=1.0, gate=None):
    T, D = x.shape
    assert T % tm == 0 and n_out % tn == 0 and q_cols % tn == 0 and n_out <= w.shape[2]
    in_specs = [
        pl.BlockSpec((tm, D), _early_next_tile(T // tm)),
        pl.BlockSpec((1, D), lambda i, j: (0, 0)),
        pl.BlockSpec((None, D, tn), lambda i, j: (layer, 0, j)),
    ]
    args = [x, gain.reshape(1, D), w]
    out_shape = [jax.ShapeDtypeStruct((T, n_out), out_dtype)]
    out_specs = [pl.BlockSpec((tm, tn), lambda i, j: (i, j))]
    if gate is not None:
        in_specs += [pl.BlockSpec((D, LANES), lambda i, j: (0, 0)),
                     pl.BlockSpec((1, LANES), lambda i, j: (0, 0))]
        args += list(gate)
        out_shape.append(jax.ShapeDtypeStruct((T, LANES), F32))
        out_specs.append(pl.BlockSpec((tm, LANES), lambda i, j: (i, 0)))
    res = pl.pallas_call(
        functools.partial(_rms_proj_kernel, tn=tn, q_cols=q_cols, q_scale=q_scale,
                          with_gate=gate is not None),
        grid=(T // tm, n_out // tn),
        in_specs=in_specs,
        out_specs=out_specs,
        out_shape=out_shape,
        scratch_shapes=[pltpu.VMEM((tm, D), BF16)],
        compiler_params=_params(),
        name="rms_proj_gate" if gate is not None else "rms_proj",
    )(*args)
    return res if gate is not None else res[0]


def _ffn_chunk(h_ref, base_ref, o_ref, wg, wu, wo):
    h = h_ref[...]
    gate = jnp.dot(h, wg, preferred_element_type=F32)
    up = jnp.dot(h, wu, preferred_element_type=F32)
    act = (gate * jax.nn.sigmoid(gate) * (0.5 * up)).astype(BF16)
    o_ref[...] = base_ref[...] + jnp.dot(act, wo, preferred_element_type=F32)


def _ffn_tile_step(j, last, x_ref, g_ref, h_ref, o_ref, weights, fg_ref):
    @pl.when(j == 0)
    def _():
        h_ref[...] = _rmsnorm(x_ref[...], g_ref[...]).astype(BF16)
        _ffn_chunk(h_ref, x_ref, o_ref, *weights())

    @pl.when(j > 0 if fg_ref is None else jnp.logical_and(j > 0, j < last))
    def _():
        _ffn_chunk(h_ref, o_ref, o_ref, *weights())

    if fg_ref is not None:
        @pl.when(j == last)
        def _():
            _ffn_chunk(h_ref, o_ref, o_ref, *weights())
            o_ref[...] = _rmsnorm(o_ref[...], fg_ref[...])


def _ffn_head_kernel(x_ref, g_ref, wg_ref, wu_ref, wo_ref, *rest, final_norm):
    if final_norm:
        fg_ref, o_ref, wg_out, wu_out, wo_out, h_ref = rest
    else:
        fg_ref = None
        o_ref, wg_out, wu_out, wo_out, h_ref = rest

    def weights():
        wg, wu, wo = (w[...].astype(BF16) for w in (wg_ref, wu_ref, wo_ref))
        wg_out[...] = wg
        wu_out[...] = wu
        wo_out[...] = wo
        return wg, wu, wo

    _ffn_tile_step(pl.program_id(0), pl.num_programs(0) - 1, x_ref, g_ref, h_ref, o_ref, weights,
                   fg_ref)


def _ffn_main_kernel(x_ref, g_ref, wg_ref, wu_ref, wo_ref, head_ref, *rest, final_norm, n_head):
    if final_norm:
        fg_ref, o_ref, h_ref = rest
    else:
        fg_ref = None
        o_ref, h_ref = rest
    i = pl.program_id(0)
    j = pl.program_id(1)
    last = pl.num_programs(1) - 1

    @pl.when(jnp.logical_and(i < n_head, j == last))
    def _():
        o_ref[...] = head_ref[...]

    @pl.when(i >= n_head)
    def _():
        _ffn_tile_step(j, last, x_ref, g_ref, h_ref, o_ref,
                       lambda: (wg_ref[...], wu_ref[...], wo_ref[...]), fg_ref)


def _ffn(x, gain, w_in, w_out, layer, *, tm, tf, head_tm, head_tf, final_gain=None):
    T, D = x.shape
    F = w_out.shape[1]
    assert T % tm == 0 and F % tf == 0 and F % head_tf == 0
    assert head_tm % tm == 0 and head_tm <= T
    final = final_gain is not None
    gain2 = gain.reshape(1, D)
    extra_specs = [pl.BlockSpec((1, D), lambda *_: (0, 0))] if final else []
    extra_args = [final_gain.reshape(1, D)] if final else []

    nfh = F // head_tf
    per = tf // head_tf
    assert tf % head_tf == 0
    once = dict(pipeline_mode=pl.Buffered(1))
    head, wg, wu, wo = pl.pallas_call(
        functools.partial(_ffn_head_kernel, final_norm=final),
        grid=(nfh,),
        in_specs=[
            pl.BlockSpec((head_tm, D), lambda j: (0, 0), **once),
            pl.BlockSpec((1, D), lambda j: (0, 0)),
            pl.BlockSpec((None, D, head_tf), lambda j: (layer, 0, j)),
            pl.BlockSpec((None, D, head_tf), lambda j: (layer, 0, j + nfh)),
            pl.BlockSpec((None, head_tf, D), lambda j: (layer, j, 0)),
        ] + extra_specs,
        out_specs=[
            pl.BlockSpec((head_tm, D), lambda j: (0, 0), **once),
            pl.BlockSpec((None, D, head_tf), lambda j: (j // per, 0, j % per)),
            pl.BlockSpec((None, D, head_tf), lambda j: (j // per, 0, j % per)),
            pl.BlockSpec((head_tf, D), lambda j: (j, 0)),
        ],
        out_shape=[
            jax.ShapeDtypeStruct((head_tm, D), F32),
            jax.ShapeDtypeStruct((F // tf, D, tf), BF16),
            jax.ShapeDtypeStruct((F // tf, D, tf), BF16),
            jax.ShapeDtypeStruct((F, D), BF16),
        ],
        scratch_shapes=[pltpu.VMEM((head_tm, D), BF16)],
        compiler_params=_params(),
        name="ffn_head_final" if final else "ffn_head",
    )(x, gain2, w_in, w_in, w_out, *extra_args)

    n_head = head_tm // tm

    def chunk(i, j):
        return jnp.where(i < n_head, 0, j)

    return pl.pallas_call(
        functools.partial(_ffn_main_kernel, final_norm=final, n_head=n_head),
        grid=(T // tm, F // tf),
        in_specs=[
            pl.BlockSpec((tm, D), lambda i, j: (i, 0)),
            pl.BlockSpec((1, D), lambda i, j: (0, 0)),
            pl.BlockSpec((None, D, tf), lambda i, j: (chunk(i, j), 0, 0)),
            pl.BlockSpec((None, D, tf), lambda i, j: (chunk(i, j), 0, 0)),
            pl.BlockSpec((tf, D), lambda i, j: (chunk(i, j), 0)),
            pl.BlockSpec((tm, D), lambda i, j: (jnp.minimum(i, n_head - 1), 0)),
        ] + extra_specs,
        out_specs=pl.BlockSpec((tm, D), lambda i, j: (i, 0)),
        out_shape=jax.ShapeDtypeStruct((T, D), F32),
        scratch_shapes=[pltpu.VMEM((tm, D), BF16)],
        compiler_params=_params(),
        name="ffn_final" if final else "ffn",
    )(x, gain2, wg, wu, wo, head, *extra_args)


def _cumsum_kernel(tri_ref, x_ref, ccol_ref, crow_ref, *, heads):
    tri = tri_ref[...]
    blk = tri.shape[0]
    carry = jnp.zeros((1, LANES), F32)
    for r0 in range(0, x_ref.shape[1], blk):
        pieces = _split3(x_ref[0, r0:r0 + blk, :] * LOG2E)
        c = carry + sum(jnp.dot(tri, p, preferred_element_type=F32) for p in pieces)
        ccol_ref[0, r0:r0 + blk, :] = c
        carry = c[blk - 1:blk, :]
    crow_ref[0] = ccol_ref[0].T[:heads]


def _fox_cumsum(logf, heads, blk=256):
    B, S, _ = logf.shape
    blk = min(blk, S)
    assert S % blk == 0
    tri = jnp.tril(jnp.ones((blk, blk), BF16))
    return pl.pallas_call(
        functools.partial(_cumsum_kernel, heads=heads),
        grid=(B,),
        in_specs=[pl.BlockSpec((blk, blk), lambda b: (0, 0)),
                  pl.BlockSpec((1, S, LANES), lambda b: (b, 0, 0))],
        out_specs=[pl.BlockSpec((1, S, LANES), lambda b: (b, 0, 0)),
                   pl.BlockSpec((1, heads, S), lambda b: (b, 0, 0))],
        out_shape=[jax.ShapeDtypeStruct((B, S, LANES), F32),
                   jax.ShapeDtypeStruct((B, heads, S), F32)],
        compiler_params=_params(),
        name="fox_cumsum",
    )(tri, logf)


def _fox_attn_kernel(q_ref, k_ref, v_ref, ccol_ref, crow_ref, o_ref, *, tile, hpb):
    hb = pl.program_id(1)
    qi = pl.program_id(2)
    lane = lax.broadcasted_iota(jnp.int32, (tile, LANES), 1)
    ccol = ccol_ref[0]
    heads = range(hpb)
    q = [q_ref[0, :, e * HEAD_DIM:(e + 1) * HEAD_DIM] for e in heads]
    cq = [jnp.sum(jnp.where(lane == hb * hpb + e, ccol, 0.0), axis=1, keepdims=True)
          for e in heads]

    def step(j, carry, diagonal):
        start = pl.multiple_of(j * tile, tile)
        out = []
        for e in heads:
            m, l, acc = carry[e]
            k = k_ref[0, pl.ds(start, tile), e * HEAD_DIM:(e + 1) * HEAD_DIM]
            v = v_ref[0, pl.ds(start, tile), e * HEAD_DIM:(e + 1) * HEAD_DIM]
            ck = crow_ref[0, e, pl.ds(j, 1), :]
            s = lax.dot_general(q[e], k, _NT, preferred_element_type=F32) - ck
            if diagonal:
                row = lax.broadcasted_iota(jnp.int32, (tile, tile), 0)
                col = lax.broadcasted_iota(jnp.int32, (tile, tile), 1)
                s = jnp.where(col <= row, s, NEG_INF)
            m_new = jnp.maximum(m, jnp.max(s, axis=1, keepdims=True) + cq[e])
            alpha = jnp.exp2(m - m_new)
            p = jnp.exp2(s - (m_new - cq[e]))
            l = alpha * l + jnp.sum(p, axis=1, keepdims=True)
            acc = alpha * acc + jnp.dot(p.astype(BF16), v, preferred_element_type=F32)
            out.append((m_new, l, acc))
        return tuple(out)

    init = tuple((jnp.full((tile, 1), NEG_INF, F32), jnp.zeros((tile, 1), F32),
                  jnp.zeros((tile, HEAD_DIM), F32)) for _ in heads)
    carry = lax.fori_loop(0, qi, lambda j, c: step(j, c, False), init)
    carry = step(qi, carry, True)
    for e in heads:
        _, l, acc = carry[e]
        o_ref[0, :, e * HEAD_DIM:(e + 1) * HEAD_DIM] = (acc / l).astype(o_ref.dtype)


def _fox_attention(qkv, ccol, crow, *, heads, tile, hpb=4):
    B, S, _ = qkv.shape
    hpb = min(hpb, heads)
    assert S % tile == 0 and heads % hpb == 0
    nq = S // tile
    nhb = heads // hpb
    w = hpb * HEAD_DIM
    crow4 = crow.reshape(B, heads, nq, tile)
    return pl.pallas_call(
        functools.partial(_fox_attn_kernel, tile=tile, hpb=hpb),
        grid=(B, nhb, nq),
        in_specs=[
            pl.BlockSpec((1, tile, w), lambda b, h, i: (b, i, h)),
            pl.BlockSpec((1, S, w), lambda b, h, i: (b, 0, nhb + h)),
            pl.BlockSpec((1, S, w), lambda b, h, i: (b, 0, 2 * nhb + h)),
            pl.BlockSpec((1, tile, LANES), lambda b, h, i: (b, i, 0)),
            pl.BlockSpec((1, hpb, nq, tile), lambda b, h, i: (b, h, 0, 0)),
        ],
        out_specs=pl.BlockSpec((1, tile, w), lambda b, h, i: (b, i, h)),
        out_shape=jax.ShapeDtypeStruct((B, S, heads * HEAD_DIM), BF16),
        compiler_params=_params(),
        name="fox_attn",
    )(qkv, qkv, qkv, ccol, crow4)


def _t5_bucket(dist):
    max_exact = NUM_BUCKETS // 2
    d = np.maximum(dist, 1).astype(np.float32)
    large = max_exact + (np.log(d / max_exact) / np.log(MAX_DISTANCE / max_exact)
                         * (NUM_BUCKETS - max_exact)).astype(np.int32)
    large = np.minimum(large, NUM_BUCKETS - 1)
    return np.where(dist < max_exact, dist, large).astype(np.int32)


def _dil_bias_tables(rel_bias, patterns, group_heads):
    period = 4 * DIL_TILE
    c = np.arange(period)
    back = c < 2 * DIL_TILE
    groups = []
    for g, (window, dil) in enumerate(patterns):
        assert window // dil == DIL_TILE
        strips = []
        for t in (0, 1):
            delta = np.where(back, t * DIL_TILE - c, t * DIL_TILE + period - c)
            valid = (delta >= 0) & (delta <= DIL_TILE) & (back | (c > period - DIL_TILE))
            bucket = _t5_bucket(np.clip(delta, 0, DIL_TILE) * dil)
            vec = rel_bias[bucket][:, g * group_heads:(g + 1) * group_heads].astype(F32)
            vec = jnp.where(valid[:, None], vec, NEG_INF).T
            flat = jnp.tile(vec, (1, DIL_TILE))[:, :DIL_TILE * (period - 1)]
            strips.append(flat.reshape(group_heads, DIL_TILE, period - 1)[:, :, :2 * DIL_TILE])
        groups.append(jnp.stack(strips, axis=1))
    return jnp.stack(groups)


def _dil_attn_kernel(*refs, dils, seq):
    n_g = len(dils)
    q_refs, k_refs, v_refs = refs[:n_g], refs[n_g:2 * n_g], refs[2 * n_g:3 * n_g]
    t_ref = refs[3 * n_g]
    o_refs = refs[3 * n_g + 1:4 * n_g + 1]
    o_sc, lse_sc = refs[4 * n_g + 1:]

    def rows(start, size, stride):
        return pl.ds(start, size) if stride == 1 else pl.ds(start, size, stride=stride)

    for g, d in enumerate(dils):
        cls_len = seq // d
        n_keys = min(2 * DIL_TILE, cls_len)
        for r in range(d):
            for u0 in range(0, cls_len, DIL_TILE):
                s0 = min(max(u0 - DIL_TILE, 0), cls_len - n_keys)
                strip = (u0 - s0) // DIL_TILE
                q_rows = rows(r + d * u0, DIL_TILE, d)
                k_rows = rows(r + d * s0, n_keys, d)
                q = q_refs[g][0, q_rows, :].astype(BF16)
                k = k_refs[g][0, k_rows, :].astype(BF16)
                v = v_refs[g][0, k_rows, :].astype(BF16)
                s = lax.dot_general(q, k, _NT, preferred_element_type=F32)
                s = s + t_ref[g, 0, strip][:, :n_keys]
                m = jnp.max(s, axis=1, keepdims=True)
                p = jnp.exp(s - m)
                l = jnp.sum(p, axis=1, keepdims=True)
                o = jnp.dot(p.astype(BF16), v, preferred_element_type=F32) / l
                o_sc[g, q_rows, :] = o
                lse_sc[g, q_rows, :] = jnp.broadcast_to(m + jnp.log(l), (DIL_TILE, LANES))

    lse = [lse_sc[g] for g in range(n_g)]
    mx = functools.reduce(jnp.maximum, lse)
    e = [jnp.exp(x - mx) for x in lse]
    inv = 1.0 / functools.reduce(lambda x, y: x + y, e)
    for g in range(n_g):
        o_refs[g][0] = (o_sc[g] * (e[g] * inv)).astype(o_refs[g].dtype)


def _dil_attention(qkv, tables, *, patterns, group_heads):
    B, S, _ = qkv.shape
    n_g = len(patterns)
    dils = tuple(d for _, d in patterns)
    assert all(S % (d * DIL_TILE) == 0 for d in dils)
    nh = n_g * group_heads

    def head_spec(part, g):
        return pl.BlockSpec((1, S, HEAD_DIM),
                            lambda b, h, part=part, g=g: (b, 0, part * nh + g * group_heads + h))

    in_specs = [head_spec(part, g) for part in range(3) for g in range(n_g)]
    in_specs.append(pl.BlockSpec((n_g, 1, 2, DIL_TILE, 2 * DIL_TILE),
                                 lambda b, h: (0, h, 0, 0, 0)))
    out_spec = pl.BlockSpec((1, S, HEAD_DIM), lambda b, h: (b, 0, h))
    return pl.pallas_call(
        functools.partial(_dil_attn_kernel, dils=dils, seq=S),
        grid=(B, group_heads),
        in_specs=in_specs,
        out_specs=[out_spec] * n_g,
        out_shape=[jax.ShapeDtypeStruct((B, S, group_heads * HEAD_DIM), BF16)] * n_g,
        scratch_shapes=[pltpu.VMEM((n_g, S, HEAD_DIM), F32), pltpu.VMEM((n_g, S, LANES), F32)],
        compiler_params=_params(),
        name="dil_attn",
    )(*([qkv] * (3 * n_g)), tables)


def _mix_cross_kernel(*refs, n_a, heads, scale):
    a_refs, wm_refs = refs[:n_a], refs[n_a:2 * n_a]
    x_ref, g_ref, wq_ref, kv_ref, wo_ref, o_ref = refs[2 * n_a:]
    x = x_ref[0]
    for a_ref, wm_ref in zip(a_refs, wm_refs):
        x = x + jnp.dot(a_ref[0], wm_ref[...], preferred_element_type=F32)
    h = _rmsnorm(x, g_ref[...]).astype(BF16)
    q = (jnp.dot(h, wq_ref[...].astype(BF16), preferred_element_type=F32) * scale).astype(BF16)
    kv = kv_ref[0]
    outs = []
    for hd in range(heads):
        qh = q[:, hd * HEAD_DIM:(hd + 1) * HEAD_DIM]
        kh = kv[:, hd * HEAD_DIM:(hd + 1) * HEAD_DIM]
        vh = kv[:, (heads + hd) * HEAD_DIM:(heads + hd + 1) * HEAD_DIM]
        s = lax.dot_general(qh, kh, _NT, preferred_element_type=F32)
        p = jnp.exp(s - jnp.max(s, axis=1, keepdims=True))
        l = jnp.sum(p, axis=1, keepdims=True)
        outs.append((jnp.dot(p.astype(BF16), vh, preferred_element_type=F32) / l).astype(BF16))
    o = jnp.concatenate(outs, axis=1)
    o_ref[0] = x + jnp.dot(o, wo_ref[...].astype(BF16), preferred_element_type=F32)


def _mix_cross(x, a_list, w_mix, mix_layer, gain, wq, kv, wo, layer, *, heads, tm):
    B, S, D = x.shape
    M = kv.shape[1]
    hd = heads * HEAD_DIM
    kg = a_list[0].shape[2]
    n_a = len(a_list)
    assert S % tm == 0 and kg * n_a == w_mix.shape[1]
    once = dict(pipeline_mode=pl.Buffered(1))
    in_specs = [pl.BlockSpec((1, tm, kg), lambda b, i: (b, i, 0)) for _ in a_list]
    in_specs += [pl.BlockSpec((None, kg, D), lambda b, i, g=g: (mix_layer, g, 0), **once)
                 for g in range(n_a)]
    in_specs += [
        pl.BlockSpec((1, tm, D), lambda b, i: (b, i, 0)),
        pl.BlockSpec((1, D), lambda b, i: (0, 0)),
        pl.BlockSpec((None, D, hd), lambda b, i: (layer, 0, 0), **once),
        pl.BlockSpec((1, M, 2 * hd), lambda b, i: (b, 0, 0)),
        pl.BlockSpec((None, hd, D), lambda b, i: (layer, 0, 0), **once),
    ]
    return pl.pallas_call(
        functools.partial(_mix_cross_kernel, n_a=n_a, heads=heads, scale=HEAD_DIM ** -0.5),
        grid=(B, S // tm),
        in_specs=in_specs,
        out_specs=pl.BlockSpec((1, tm, D), lambda b, i: (b, i, 0)),
        out_shape=jax.ShapeDtypeStruct((B, S, D), F32),
        compiler_params=_params(),
        name="mix_cross",
    )(*a_list, *([w_mix] * n_a), x, gain.reshape(1, D), wq, kv, wo)


def _fox_mixer(x, gain, w_in, w_gate, b_f, layer, *, heads, attn_tile, tm):
    B, S, D = x.shape
    hd = heads * HEAD_DIM
    w_f = jnp.pad(w_gate, ((0, 0), (0, LANES - heads))).astype(BF16)
    b_pad = jnp.pad(b_f, (0, LANES - heads)).reshape(1, LANES).astype(F32)
    qkv, logf = _rms_proj(x.reshape(B * S, D), gain, w_in, layer, n_out=3 * hd, tm=tm,
                          tn=min(1024, hd), out_dtype=BF16, q_cols=hd,
                          q_scale=HEAD_DIM ** -0.5 * LOG2E, gate=(w_f, b_pad))
    ccol, crow = _fox_cumsum(logf.reshape(B, S, LANES), heads)
    return [_fox_attention(qkv.reshape(B, S, 3 * hd), ccol, crow, heads=heads, tile=attn_tile)]


def _dilated_mixer(x, gain, w_in, layer, rel_bias, *, patterns, group_heads, tm):
    B, S, D = x.shape
    n_g = len(patterns)
    hd = n_g * group_heads * HEAD_DIM
    qkv = _rms_proj(x.reshape(B * S, D), gain, w_in, layer, n_out=3 * hd, tm=tm,
                    tn=group_heads * HEAD_DIM, out_dtype=F32, q_cols=hd, q_scale=HEAD_DIM ** -0.5)
    tables = _dil_bias_tables(rel_bias, patterns, group_heads)
    return _dil_attention(qkv.reshape(B, S, 3 * hd), tables, patterns=patterns,
                          group_heads=group_heads)


def _forward(x, mem, ffn1_norm, ffn1_w_in, ffn1_w_out, mix_norm, fox_w_in, fox_b_f, fox_w_out,
             dil_w_in, dil_w_out, rel_bias, cross_norm, mem_norm, cross_w_q, cross_w_kv,
             cross_w_out, ffn2_norm, ffn2_w_in, ffn2_w_out, final_norm, *,
             patterns, fox_heads, group_heads, cross_heads, tm, tf, head_tm, head_tf, proj_tm,
             attn_tile, cross_tm):
    B, S, D = x.shape
    M = mem.shape[1]
    depth = ffn1_norm.shape[0]
    n_mixers = 2
    fox_hd = fox_heads * HEAD_DIM
    cross_hd = cross_heads * HEAD_DIM
    fox_w_gate = fox_w_in[:, :, 3 * fox_hd:]
    fox_w_in, fox_w_out, dil_w_in, dil_w_out, cross_w_kv = (
        w.astype(BF16) for w in (fox_w_in, fox_w_out, dil_w_in, dil_w_out, cross_w_kv))
    ffn_tiles = dict(tm=tm, tf=tf, head_tm=head_tm, head_tf=head_tf)
    for i in range(depth):
        x = _ffn(x.reshape(B * S, D), ffn1_norm[i], ffn1_w_in, ffn1_w_out, i,
                 **ffn_tiles).reshape(B, S, D)
        j = i // n_mixers
        if i % n_mixers == 0:
            mixed = _fox_mixer(x, mix_norm[i], fox_w_in, fox_w_gate[j], fox_b_f[j], j,
                               heads=fox_heads, attn_tile=attn_tile, tm=proj_tm)
            w_mix = fox_w_out
        else:
            mixed = _dilated_mixer(x, mix_norm[i], dil_w_in, j, rel_bias,
                                   patterns=patterns, group_heads=group_heads, tm=proj_tm)
            w_mix = dil_w_out
        kv = _rms_proj(mem.reshape(B * M, D), mem_norm, cross_w_kv, i, n_out=2 * cross_hd,
                       tm=min(tm, B * M), tn=min(512, cross_hd), out_dtype=BF16)
        x = _mix_cross(x, mixed, w_mix, j, cross_norm[i], cross_w_q,
                       kv.reshape(B, M, 2 * cross_hd), cross_w_out, i,
                       heads=cross_heads, tm=cross_tm)
        x = _ffn(x.reshape(B * S, D), ffn2_norm[i], ffn2_w_in, ffn2_w_out, i, **ffn_tiles,
                 final_gain=final_norm if i == depth - 1 else None).reshape(B, S, D)
    return x


def kernel(x, mem, ffn1_norm, ffn1_w_in, ffn1_w_out, mix_norm, fox_w_in, fox_b_f, fox_w_out, dil_w_in, dil_w_out, rel_bias, cross_norm, mem_norm, cross_w_q, cross_w_kv, cross_w_out, ffn2_norm, ffn2_w_in, ffn2_w_out, final_norm):
    return _forward(x, mem, ffn1_norm, ffn1_w_in, ffn1_w_out, mix_norm, fox_w_in, fox_b_f,
                    fox_w_out, dil_w_in, dil_w_out, rel_bias, cross_norm, mem_norm, cross_w_q,
                    cross_w_kv, cross_w_out, ffn2_norm, ffn2_w_in, ffn2_w_out, final_norm,
                    patterns=DIL_PATTERNS, fox_heads=FOX_HEADS, group_heads=DIL_GROUP_HEADS,
                    cross_heads=CROSS_HEADS, tm=512, tf=512, head_tm=1024, head_tf=256,
                    proj_tm=1024, attn_tile=512, cross_tm=512)
```

```python
import functools

import numpy as np
import jax
import jax.numpy as jnp
from jax import lax
from jax.experimental import pallas as pl
from jax.experimental.pallas import tpu as pltpu

HEAD_DIM = 128
FOX_HEADS = 16
DIL_PATTERNS = ((128, 1), (512, 4), (2048, 16))
DIL_GROUP_HEADS = 6
CROSS_HEADS = 4
NUM_BUCKETS = 32
MAX_DISTANCE = 2048
RMS_EPS = 1e-6
NEG_INF = -1e30

LANES = 128
DIL_TILE = 128
VMEM_LIMIT_BYTES = 56 * 1024 * 1024

BF16 = jnp.bfloat16
F32 = jnp.float32
_NT = (((1,), (1,)), ((), ()))
LOG2E = 1.4426950408889634


def _params():
    return pltpu.CompilerParams(vmem_limit_bytes=VMEM_LIMIT_BYTES)


def _rmsnorm(x, g):
    return x * lax.rsqrt(jnp.mean(x * x, axis=-1, keepdims=True) + RMS_EPS) * g


def _log_sigmoid(z):
    return jnp.minimum(z, 0.0) - jnp.log1p(jnp.exp(-jnp.abs(z)))


def _early_next_tile(n_tiles, switch=1):
    return lambda i, j: (jnp.where(j < switch, i, jnp.minimum(i + 1, n_tiles - 1)), 0)


def _split3(x):
    hi = x.astype(BF16)
    r = x - hi.astype(F32)
    mid = r.astype(BF16)
    lo = (r - mid.astype(F32)).astype(BF16)
    return hi, mid, lo


def _rms_proj_kernel(x_ref, g_ref, w_ref, *rest, tn, q_cols, q_scale, with_gate):
    if with_gate:
        wf_ref, bf_ref, o_ref, logf_ref, h_ref = rest
    else:
        o_ref, h_ref = rest
    j = pl.program_id(1)

    @pl.when(j == 0)
    def _():
        h = _rmsnorm(x_ref[...], g_ref[...]).astype(BF16)
        h_ref[...] = h
        if with_gate:
            z = jnp.dot(h, wf_ref[...], preferred_element_type=F32) + bf_ref[...]
            logf_ref[...] = _log_sigmoid(z)

    acc = jnp.dot(h_ref[...], w_ref[...], preferred_element_type=F32)
    if q_cols:
        acc = acc * jnp.where(j * tn < q_cols, q_scale, 1.0).astype(F32)
    o_ref[...] = acc.astype(o_ref.dtype)


def _rms_proj(x, gain, w, layer, *, n_out, tm, tn, out_dtype, q_cols=0, q_scale=1.0, gate=None):
    T, D = x.shape
    assert T % tm == 0 and n_out % tn == 0 and q_cols % tn == 0 and n_out <= w.shape[2]
    in_specs = [
        pl.BlockSpec((tm, D), _early_next_tile(T // tm)),
        pl.BlockSpec((1, D), lambda i, j: (0, 0)),
        pl.BlockSpec((None, D, tn), lambda i, j: (layer, 0, j)),
    ]
    args = [x, gain.reshape(1, D), w]
    out_shape = [jax.ShapeDtypeStruct((T, n_out), out_dtype)]
    out_specs = [pl.BlockSpec((tm, tn), lambda i, j: (i, j))]
    if gate is not None:
        in_specs += [pl.BlockSpec((D, LANES), lambda i, j: (0, 0)),
                     pl.BlockSpec((1, LANES), lambda i, j: (0, 0))]
        args += list(gate)
        out_shape.append(jax.ShapeDtypeStruct((T, LANES), F32))
        out_specs.append(pl.BlockSpec((tm, LANES), lambda i, j: (i, 0)))
    res = pl.pallas_call(
        functools.partial(_rms_proj_kernel, tn=tn, q_cols=q_cols, q_scale=q_scale,
                          with_gate=gate is not None),
        grid=(T // tm, n_out // tn),
        in_specs=in_specs,
        out_specs=out_specs,
        out_shape=out_shape,
        scratch_shapes=[pltpu.VMEM((tm, D), BF16)],
        compiler_params=_params(),
        name="rms_proj_gate" if gate is not None else "rms_proj",
    )(*args)
    return res if gate is not None else res[0]


def _ffn_start(x_ref, g_ref, h_ref, o_ref):
    x = x_ref[...]
    h_ref[...] = _rmsnorm(x, g_ref[...]).astype(BF16)
    o_ref[...] = x


def _ffn_chunk(h_ref, base_ref, o_ref, wg, wu, wo):
    h = h_ref[...]
    gate = jnp.dot(h, wg, preferred_element_type=F32)
    up = jnp.dot(h, wu, preferred_element_type=F32)
    act = (gate * jax.nn.sigmoid(gate) * (0.5 * up)).astype(BF16)
    o_ref[...] = base_ref[...] + jnp.dot(act, wo, preferred_element_type=F32)


def _ffn_head_kernel(x_ref, g_ref, wg_ref, wu_ref, wo_ref, *rest, final_norm):
    if final_norm:
        fg_ref, o_ref, wg_out, wu_out, wo_out, h_ref = rest
    else:
        o_ref, wg_out, wu_out, wo_out, h_ref = rest
    j = pl.program_id(0)

    @pl.when(j == 0)
    def _():
        _ffn_start(x_ref, g_ref, h_ref, o_ref)

    wg, wu, wo = (w[...].astype(BF16) for w in (wg_ref, wu_ref, wo_ref))
    wg_out[...] = wg
    wu_out[...] = wu
    wo_out[...] = wo
    _ffn_chunk(h_ref, o_ref, o_ref, wg, wu, wo)

    if final_norm:
        @pl.when(j == pl.num_programs(0) - 1)
        def _():
            o_ref[...] = _rmsnorm(o_ref[...], fg_ref[...])


def _ffn_main_kernel(x_ref, g_ref, wg_ref, wu_ref, wo_ref, head_ref, *rest, final_norm, n_head):
    if final_norm:
        fg_ref, o_ref, h_ref = rest
    else:
        o_ref, h_ref = rest
    i = pl.program_id(0)
    j = pl.program_id(1)
    last = pl.num_programs(1) - 1

    @pl.when(jnp.logical_and(i < n_head, j == last))
    def _():
        o_ref[...] = head_ref[...]

    @pl.when(i >= n_head)
    def _():
        @pl.when(j == 0)
        def _():
            h_ref[...] = _rmsnorm(x_ref[...], g_ref[...]).astype(BF16)
            _ffn_chunk(h_ref, x_ref, o_ref, wg_ref[...], wu_ref[...], wo_ref[...])

        @pl.when(j > 0)
        def _():
            _ffn_chunk(h_ref, o_ref, o_ref, wg_ref[...], wu_ref[...], wo_ref[...])

        if final_norm:
            @pl.when(j == last)
            def _():
                o_ref[...] = _rmsnorm(o_ref[...], fg_ref[...])


def _ffn(x, gain, w_in, w_out, layer, *, tm, tf, head_tm, head_tf, final_gain=None):
    T, D = x.shape
    F = w_out.shape[1]
    assert T % tm == 0 and F % tf == 0 and F % head_tf == 0
    assert head_tm % tm == 0 and head_tm <= T
    final = final_gain is not None
    gain2 = gain.reshape(1, D)
    extra_specs = [pl.BlockSpec((1, D), lambda *_: (0, 0))] if final else []
    extra_args = [final_gain.reshape(1, D)] if final else []

    nfh = F // head_tf
    per = tf // head_tf
    assert tf % head_tf == 0
    once = dict(pipeline_mode=pl.Buffered(1))
    head, wg, wu, wo = pl.pallas_call(
        functools.partial(_ffn_head_kernel, final_norm=final),
        grid=(nfh,),
        in_specs=[
            pl.BlockSpec((head_tm, D), lambda j: (0, 0), **once),
            pl.BlockSpec((1, D), lambda j: (0, 0)),
            pl.BlockSpec((None, D, head_tf), lambda j: (layer, 0, j)),
            pl.BlockSpec((None, D, head_tf), lambda j: (layer, 0, j + nfh)),
            pl.BlockSpec((None, head_tf, D), lambda j: (layer, j, 0)),
        ] + extra_specs,
        out_specs=[
            pl.BlockSpec((head_tm, D), lambda j: (0, 0), **once),
            pl.BlockSpec((None, D, head_tf), lambda j: (j // per, 0, j % per)),
            pl.BlockSpec((None, D, head_tf), lambda j: (j // per, 0, j % per)),
            pl.BlockSpec((head_tf, D), lambda j: (j, 0)),
        ],
        out_shape=[
            jax.ShapeDtypeStruct((head_tm, D), F32),
            jax.ShapeDtypeStruct((F // tf, D, tf), BF16),
            jax.ShapeDtypeStruct((F // tf, D, tf), BF16),
            jax.ShapeDtypeStruct((F, D), BF16),
        ],
        scratch_shapes=[pltpu.VMEM((head_tm, D), BF16)],
        compiler_params=_params(),
        name="ffn_head_final" if final else "ffn_head",
    )(x, gain2, w_in, w_in, w_out, *extra_args)

    n_head = head_tm // tm

    def chunk(i, j):
        return jnp.where(i < n_head, 0, j)

    return pl.pallas_call(
        functools.partial(_ffn_main_kernel, final_norm=final, n_head=n_head),
        grid=(T // tm, F // tf),
        in_specs=[
            pl.BlockSpec((tm, D), lambda i, j: (i, 0)),
            pl.BlockSpec((1, D), lambda i, j: (0, 0)),
            pl.BlockSpec((None, D, tf), lambda i, j: (chunk(i, j), 0, 0)),
            pl.BlockSpec((None, D, tf), lambda i, j: (chunk(i, j), 0, 0)),
            pl.BlockSpec((tf, D), lambda i, j: (chunk(i, j), 0)),
            pl.BlockSpec((tm, D), lambda i, j: (jnp.minimum(i, n_head - 1), 0)),
        ] + extra_specs,
        out_specs=pl.BlockSpec((tm, D), lambda i, j: (i, 0)),
        out_shape=jax.ShapeDtypeStruct((T, D), F32),
        scratch_shapes=[pltpu.VMEM((tm, D), BF16)],
        compiler_params=_params(),
        name="ffn_final" if final else "ffn",
    )(x, gain2, wg, wu, wo, head, *extra_args)


def _cumsum_kernel(tri_ref, x_ref, ccol_ref, crow_ref, *, heads):
    tri = tri_ref[...]
    blk = tri.shape[0]
    carry = jnp.zeros((1, LANES), F32)
    for r0 in range(0, x_ref.shape[1], blk):
        pieces = _split3(x_ref[0, r0:r0 + blk, :] * LOG2E)
        c = carry + sum(jnp.dot(tri, p, preferred_element_type=F32) for p in pieces)
        ccol_ref[0, r0:r0 + blk, :] = c
        carry = c[blk - 1:blk, :]
    crow_ref[0] = ccol_ref[0].T[:heads]


def _fox_cumsum(logf, heads, blk=256):
    B, S, _ = logf.shape
    blk = min(blk, S)
    assert S % blk == 0
    tri = jnp.tril(jnp.ones((blk, blk), BF16))
    return pl.pallas_call(
        functools.partial(_cumsum_kernel, heads=heads),
        grid=(B,),
        in_specs=[pl.BlockSpec((blk, blk), lambda b: (0, 0)),
                  pl.BlockSpec((1, S, LANES), lambda b: (b, 0, 0))],
        out_specs=[pl.BlockSpec((1, S, LANES), lambda b: (b, 0, 0)),
                   pl.BlockSpec((1, heads, S), lambda b: (b, 0, 0))],
        out_shape=[jax.ShapeDtypeStruct((B, S, LANES), F32),
                   jax.ShapeDtypeStruct((B, heads, S), F32)],
        compiler_params=_params(),
        name="fox_cumsum",
    )(tri, logf)


def _fox_attn_kernel(q_ref, k_ref, v_ref, ccol_ref, crow_ref, o_ref, *, tile, hpb):
    hb = pl.program_id(1)
    lane = lax.broadcasted_iota(jnp.int32, (tile, LANES), 1)
    heads = range(hpb)
    for qi in range(q_ref.shape[1] // tile):
        rows = slice(qi * tile, (qi + 1) * tile)
        ccol = ccol_ref[0, rows, :]
        q = [q_ref[0, rows, e * HEAD_DIM:(e + 1) * HEAD_DIM] for e in heads]
        cq = [jnp.sum(jnp.where(lane == hb * hpb + e, ccol, 0.0), axis=1, keepdims=True)
              for e in heads]
        carry = [(jnp.full((tile, 1), NEG_INF, F32), jnp.zeros((tile, 1), F32),
                  jnp.zeros((tile, HEAD_DIM), F32)) for _ in heads]
        for j in range(qi + 1):
            keys = slice(j * tile, (j + 1) * tile)
            for e in heads:
                m, l, acc = carry[e]
                k = k_ref[0, keys, e * HEAD_DIM:(e + 1) * HEAD_DIM]
                v = v_ref[0, keys, e * HEAD_DIM:(e + 1) * HEAD_DIM]
                ck = crow_ref[0, e, j:j + 1, :]
                s = lax.dot_general(q[e], k, _NT, preferred_element_type=F32) - ck
                if j == qi:
                    row = lax.broadcasted_iota(jnp.int32, (tile, tile), 0)
                    col = lax.broadcasted_iota(jnp.int32, (tile, tile), 1)
                    s = jnp.where(col <= row, s, NEG_INF)
                m_new = jnp.maximum(m, jnp.max(s, axis=1, keepdims=True) + cq[e])
                alpha = jnp.exp2(m - m_new)
                p = jnp.exp2(s - (m_new - cq[e]))
                l = alpha * l + jnp.sum(p, axis=1, keepdims=True)
                acc = alpha * acc + jnp.dot(p.astype(BF16), v, preferred_element_type=F32)
                carry[e] = (m_new, l, acc)
        for e in heads:
            _, l, acc = carry[e]
            o_ref[0, rows, e * HEAD_DIM:(e + 1) * HEAD_DIM] = (acc / l).astype(o_ref.dtype)


def _fox_attention(qkv, ccol, crow, *, heads, tile, hpb=2):
    B, S, _ = qkv.shape
    hpb = min(hpb, heads)
    assert S % tile == 0 and heads % hpb == 0
    nq = S // tile
    nhb = heads // hpb
    w = hpb * HEAD_DIM
    crow4 = crow.reshape(B, heads, nq, tile)
    return pl.pallas_call(
        functools.partial(_fox_attn_kernel, tile=tile, hpb=hpb),
        grid=(B, nhb),
        in_specs=[
            pl.BlockSpec((1, S, w), lambda b, h: (b, 0, h)),
            pl.BlockSpec((1, S, w), lambda b, h: (b, 0, nhb + h)),
            pl.BlockSpec((1, S, w), lambda b, h: (b, 0, 2 * nhb + h)),
            pl.BlockSpec((1, S, LANES), lambda b, h: (b, 0, 0)),
            pl.BlockSpec((1, hpb, nq, tile), lambda b, h: (b, h, 0, 0)),
        ],
        out_specs=pl.BlockSpec((1, S, w), lambda b, h: (b, 0, h)),
        out_shape=jax.ShapeDtypeStruct((B, S, heads * HEAD_DIM), BF16),
        compiler_params=_params(),
        name="fox_attn",
    )(qkv, qkv, qkv, ccol, crow4)


def _t5_bucket(dist):
    max_exact = NUM_BUCKETS // 2
    d = np.maximum(dist, 1).astype(np.float32)
    large = max_exact + (np.log(d / max_exact) / np.log(MAX_DISTANCE / max_exact)
                         * (NUM_BUCKETS - max_exact)).astype(np.int32)
    large = np.minimum(large, NUM_BUCKETS - 1)
    return np.where(dist < max_exact, dist, large).astype(np.int32)


def _dil_bias_tables(rel_bias, patterns, group_heads):
    period = 4 * DIL_TILE
    c = np.arange(period)
    back = c < 2 * DIL_TILE
    groups = []
    for g, (window, dil) in enumerate(patterns):
        assert window // dil == DIL_TILE
        strips = []
        for t in (0, 1):
            delta = np.where(back, t * DIL_TILE - c, t * DIL_TILE + period - c)
            valid = (delta >= 0) & (delta <= DIL_TILE) & (back | (c > period - DIL_TILE))
            bucket = _t5_bucket(np.clip(delta, 0, DIL_TILE) * dil)
            vec = rel_bias[bucket][:, g * group_heads:(g + 1) * group_heads].astype(F32)
            vec = jnp.where(valid[:, None], vec, NEG_INF).T
            flat = jnp.tile(vec, (1, DIL_TILE))[:, :DIL_TILE * (period - 1)]
            strips.append(flat.reshape(group_heads, DIL_TILE, period - 1)[:, :, :2 * DIL_TILE])
        groups.append(jnp.stack(strips, axis=1))
    return jnp.stack(groups)


def _dil_attn_kernel(*refs, dils, seq):
    n_g = len(dils)
    q_refs, k_refs, v_refs = refs[:n_g], refs[n_g:2 * n_g], refs[2 * n_g:3 * n_g]
    t_ref = refs[3 * n_g]
    o_refs = refs[3 * n_g + 1:4 * n_g + 1]
    o_sc, lse_sc = refs[4 * n_g + 1:]

    def rows(start, size, stride):
        return pl.ds(start, size) if stride == 1 else pl.ds(start, size, stride=stride)

    for g, d in enumerate(dils):
        cls_len = seq // d
        n_keys = min(2 * DIL_TILE, cls_len)
        for r in range(d):
            for u0 in range(0, cls_len, DIL_TILE):
                s0 = min(max(u0 - DIL_TILE, 0), cls_len - n_keys)
                strip = (u0 - s0) // DIL_TILE
                q_rows = rows(r + d * u0, DIL_TILE, d)
                k_rows = rows(r + d * s0, n_keys, d)
                q = q_refs[g][0, q_rows, :].astype(BF16)
                k = k_refs[g][0, k_rows, :].astype(BF16)
                v = v_refs[g][0, k_rows, :].astype(BF16)
                s = lax.dot_general(q, k, _NT, preferred_element_type=F32)
                s = s + t_ref[g, 0, strip][:, :n_keys]
                m = jnp.max(s, axis=1, keepdims=True)
                p = jnp.exp(s - m)
                l = jnp.sum(p, axis=1, keepdims=True)
                o = jnp.dot(p.astype(BF16), v, preferred_element_type=F32) / l
                o_sc[g, q_rows, :] = o
                lse_sc[g, q_rows, :] = jnp.broadcast_to(m + jnp.log(l), (DIL_TILE, LANES))

    lse = [lse_sc[g] for g in range(n_g)]
    mx = functools.reduce(jnp.maximum, lse)
    e = [jnp.exp(x - mx) for x in lse]
    inv = 1.0 / functools.reduce(lambda x, y: x + y, e)
    for g in range(n_g):
        o_refs[g][0] = (o_sc[g] * (e[g] * inv)).astype(o_refs[g].dtype)


def _dil_attention(qkv, tables, *, patterns, group_heads):
    B, S, _ = qkv.shape
    n_g = len(patterns)
    dils = tuple(d for _, d in patterns)
    assert all(S % (d * DIL_TILE) == 0 for d in dils)
    nh = n_g * group_heads

    def head_spec(part, g):
        return pl.BlockSpec((1, S, HEAD_DIM),
                            lambda b, h, part=part, g=g: (b, 0, part * nh + g * group_heads + h))

    in_specs = [head_spec(part, g) for part in range(3) for g in range(n_g)]
    in_specs.append(pl.BlockSpec((n_g, 1, 2, DIL_TILE, 2 * DIL_TILE),
                                 lambda b, h: (0, h, 0, 0, 0)))
    out_spec = pl.BlockSpec((1, S, HEAD_DIM), lambda b, h: (b, 0, h))
    return pl.pallas_call(
        functools.partial(_dil_attn_kernel, dils=dils, seq=S),
        grid=(B, group_heads),
        in_specs=in_specs,
        out_specs=[out_spec] * n_g,
        out_shape=[jax.ShapeDtypeStruct((B, S, group_heads * HEAD_DIM), BF16)] * n_g,
        scratch_shapes=[pltpu.VMEM((n_g, S, HEAD_DIM), F32), pltpu.VMEM((n_g, S, LANES), F32)],
        compiler_params=_params(),
        name="dil_attn",
    )(*([qkv] * (3 * n_g)), tables)


def _mix_cross_kernel(*refs, n_a, heads, scale):
    a_refs, wm_refs = refs[:n_a], refs[n_a:2 * n_a]
    x_ref, g_ref, wq_ref, kv_ref, wo_ref, o_ref = refs[2 * n_a:]
    x = x_ref[0]
    for a_ref, wm_ref in zip(a_refs, wm_refs):
        x = x + jnp.dot(a_ref[0], wm_ref[...], preferred_element_type=F32)
    h = _rmsnorm(x, g_ref[...]).astype(BF16)
    q = (jnp.dot(h, wq_ref[...].astype(BF16), preferred_element_type=F32) * scale).astype(BF16)
    kv = kv_ref[0]
    outs = []
    for hd in range(heads):
        qh = q[:, hd * HEAD_DIM:(hd + 1) * HEAD_DIM]
        kh = kv[:, hd * HEAD_DIM:(hd + 1) * HEAD_DIM]
        vh = kv[:, (heads + hd) * HEAD_DIM:(heads + hd + 1) * HEAD_DIM]
        s = lax.dot_general(qh, kh, _NT, preferred_element_type=F32)
        p = jnp.exp(s - jnp.max(s, axis=1, keepdims=True))
        l = jnp.sum(p, axis=1, keepdims=True)
        outs.append((jnp.dot(p.astype(BF16), vh, preferred_element_type=F32) / l).astype(BF16))
    o = jnp.concatenate(outs, axis=1)
    o_ref[0] = x + jnp.dot(o, wo_ref[...].astype(BF16), preferred_element_type=F32)


def _mix_cross(x, a_list, w_mix, mix_layer, gain, wq, kv, wo, layer, *, heads, tm):
    B, S, D = x.shape
    M = kv.shape[1]
    hd = heads * HEAD_DIM
    kg = a_list[0].shape[2]
    n_a = len(a_list)
    assert S % tm == 0 and kg * n_a == w_mix.shape[1]
    once = dict(pipeline_mode=pl.Buffered(1))
    in_specs = [pl.BlockSpec((1, tm, kg), lambda b, i: (b, i, 0)) for _ in a_list]
    in_specs += [pl.BlockSpec((None, kg, D), lambda b, i, g=g: (mix_layer, g, 0), **once)
                 for g in range(n_a)]
    in_specs += [
        pl.BlockSpec((1, tm, D), lambda b, i: (b, i, 0)),
        pl.BlockSpec((1, D), lambda b, i: (0, 0)),
        pl.BlockSpec((None, D, hd), lambda b, i: (layer, 0, 0), **once),
        pl.BlockSpec((1, M, 2 * hd), lambda b, i: (b, 0, 0)),
        pl.BlockSpec((None, hd, D), lambda b, i: (layer, 0, 0), **once),
    ]
    return pl.pallas_call(
        functools.partial(_mix_cross_kernel, n_a=n_a, heads=heads, scale=HEAD_DIM ** -0.5),
        grid=(B, S // tm),
        in_specs=in_specs,
        out_specs=pl.BlockSpec((1, tm, D), lambda b, i: (b, i, 0)),
        out_shape=jax.ShapeDtypeStruct((B, S, D), F32),
        compiler_params=_params(),
        name="mix_cross",
    )(*a_list, *([w_mix] * n_a), x, gain.reshape(1, D), wq, kv, wo)


def _fox_mixer(x, gain, w_in, w_gate, b_f, layer, *, heads, attn_tile, tm):
    B, S, D = x.shape
    hd = heads * HEAD_DIM
    w_f = jnp.pad(w_gate, ((0, 0), (0, LANES - heads))).astype(BF16)
    b_pad = jnp.pad(b_f, (0, LANES - heads)).reshape(1, LANES).astype(F32)
    qkv, logf = _rms_proj(x.reshape(B * S, D), gain, w_in, layer, n_out=3 * hd, tm=tm,
                          tn=min(1024, hd), out_dtype=BF16, q_cols=hd,
                          q_scale=HEAD_DIM ** -0.5 * LOG2E, gate=(w_f, b_pad))
    ccol, crow = _fox_cumsum(logf.reshape(B, S, LANES), heads)
    return [_fox_attention(qkv.reshape(B, S, 3 * hd), ccol, crow, heads=heads, tile=attn_tile)]


def _dilated_mixer(x, gain, w_in, layer, rel_bias, *, patterns, group_heads, tm):
    B, S, D = x.shape
    n_g = len(patterns)
    hd = n_g * group_heads * HEAD_DIM
    qkv = _rms_proj(x.reshape(B * S, D), gain, w_in, layer, n_out=3 * hd, tm=tm,
                    tn=group_heads * HEAD_DIM, out_dtype=F32, q_cols=hd, q_scale=HEAD_DIM ** -0.5)
    tables = _dil_bias_tables(rel_bias, patterns, group_heads)
    return _dil_attention(qkv.reshape(B, S, 3 * hd), tables, patterns=patterns,
                          group_heads=group_heads)


def _forward(x, mem, ffn1_norm, ffn1_w_in, ffn1_w_out, mix_norm, fox_w_in, fox_b_f, fox_w_out,
             dil_w_in, dil_w_out, rel_bias, cross_norm, mem_norm, cross_w_q, cross_w_kv,
             cross_w_out, ffn2_norm, ffn2_w_in, ffn2_w_out, final_norm, *,
             patterns, fox_heads, group_heads, cross_heads, tm, tf, head_tm, head_tf, proj_tm,
             attn_tile, cross_tm):
    B, S, D = x.shape
    M = mem.shape[1]
    depth = ffn1_norm.shape[0]
    n_mixers = 2
    fox_hd = fox_heads * HEAD_DIM
    cross_hd = cross_heads * HEAD_DIM
    fox_w_gate = fox_w_in[:, :, 3 * fox_hd:]
    fox_w_in, fox_w_out, dil_w_in, dil_w_out, cross_w_kv = (
        w.astype(BF16) for w in (fox_w_in, fox_w_out, dil_w_in, dil_w_out, cross_w_kv))
    ffn_tiles = dict(tm=tm, tf=tf, head_tm=head_tm, head_tf=head_tf)
    for i in range(depth):
        x = _ffn(x.reshape(B * S, D), ffn1_norm[i], ffn1_w_in, ffn1_w_out, i,
                 **ffn_tiles).reshape(B, S, D)
        j = i // n_mixers
        if i % n_mixers == 0:
            mixed = _fox_mixer(x, mix_norm[i], fox_w_in, fox_w_gate[j], fox_b_f[j], j,
                               heads=fox_heads, attn_tile=attn_tile, tm=proj_tm)
            w_mix = fox_w_out
        else:
            mixed = _dilated_mixer(x, mix_norm[i], dil_w_in, j, rel_bias,
                                   patterns=patterns, group_heads=group_heads, tm=proj_tm)
            w_mix = dil_w_out
        kv = _rms_proj(mem.reshape(B * M, D), mem_norm, cross_w_kv, i, n_out=2 * cross_hd,
                       tm=min(tm, B * M), tn=min(512, cross_hd), out_dtype=BF16)
        x = _mix_cross(x, mixed, w_mix, j, cross_norm[i], cross_w_q,
                       kv.reshape(B, M, 2 * cross_hd), cross_w_out, i,
                       heads=cross_heads, tm=cross_tm)
        x = _ffn(x.reshape(B * S, D), ffn2_norm[i], ffn2_w_in, ffn2_w_out, i, **ffn_tiles,
                 final_gain=final_norm if i == depth - 1 else None).reshape(B, S, D)
    return x


def kernel(x, mem, ffn1_norm, ffn1_w_in, ffn1_w_out, mix_norm, fox_w_in, fox_b_f, fox_w_out, dil_w_in, dil_w_out, rel_bias, cross_norm, mem_norm, cross_w_q, cross_w_kv, cross_w_out, ffn2_norm, ffn2_w_in, ffn2_w_out, final_norm):
    return _forward(x, mem, ffn1_norm, ffn1_w_in, ffn1_w_out, mix_norm, fox_w_in, fox_b_f,
                    fox_w_out, dil_w_in, dil_w_out, rel_bias, cross_norm, mem_norm, cross_w_q,
                    cross_w_kv, cross_w_out, ffn2_norm, ffn2_w_in, ffn2_w_out, final_norm,
                    patterns=DIL_PATTERNS, fox_heads=FOX_HEADS, group_heads=DIL_GROUP_HEADS,
                    cross_heads=CROSS_HEADS, tm=512, tf=512, head_tm=1024, head_tf=256,
                    proj_tm=1024, attn_tile=512, cross_tm=512)
```

```python
import functools

import numpy as np
import jax
import jax.numpy as jnp
from jax import lax
from jax.experimental import pallas as pl
from jax.experimental.pallas import tpu as pltpu

HEAD_DIM = 128
FOX_HEADS = 16
DIL_PATTERNS = ((128, 1), (512, 4), (2048, 16))
DIL_GROUP_HEADS = 6
CROSS_HEADS = 4
NUM_BUCKETS = 32
MAX_DISTANCE = 2048
RMS_EPS = 1e-6
NEG_INF = -1e30

LANES = 128
DIL_TILE = 128
VMEM_LIMIT_BYTES = 56 * 1024 * 1024

BF16 = jnp.bfloat16
F32 = jnp.float32
_NT = (((1,), (1,)), ((), ()))
LOG2E = 1.4426950408889634


def _params():
    return pltpu.CompilerParams(vmem_limit_bytes=VMEM_LIMIT_BYTES)


def _rmsnorm(x, g):
    return x * lax.rsqrt(jnp.mean(x * x, axis=-1, keepdims=True) + RMS_EPS) * g


def _log_sigmoid(z):
    return jnp.minimum(z, 0.0) - jnp.log1p(jnp.exp(-jnp.abs(z)))


def _early_next_tile(n_tiles, switch=1):
    return lambda i, j: (jnp.where(j < switch, i, jnp.minimum(i + 1, n_tiles - 1)), 0)


def _split3(x):
    hi = x.astype(BF16)
    r = x - hi.astype(F32)
    mid = r.astype(BF16)
    lo = (r - mid.astype(F32)).astype(BF16)
    return hi, mid, lo


def _rms_proj_kernel(x_ref, g_ref, w_ref, *rest, tn, q_cols, q_scale, with_gate):
    if with_gate:
        wf_ref, bf_ref, o_ref, logf_ref, h_ref = rest
    else:
        o_ref, h_ref = rest
    j = pl.program_id(1)

    @pl.when(j == 0)
    def _():
        h = _rmsnorm(x_ref[...], g_ref[...]).astype(BF16)
        h_ref[...] = h
        if with_gate:
            z = jnp.dot(h, wf_ref[...], preferred_element_type=F32) + bf_ref[...]
            logf_ref[...] = _log_sigmoid(z)

    acc = jnp.dot(h_ref[...], w_ref[...], preferred_element_type=F32)
    if q_cols:
        acc = acc * jnp.where(j * tn < q_cols, q_scale, 1.0).astype(F32)
    o_ref[...] = acc.astype(o_ref.dtype)


def _rms_proj(x, gain, w, layer, *, n_out, tm, tn, out_dtype, q_cols=0, q_scale=1.0, gate=None):
    T, D = x.shape
    assert T % tm == 0 and n_out % tn == 0 and q_cols % tn == 0 and n_out <= w.shape[2]
    in_specs = [
        pl.BlockSpec((tm, D), _early_next_tile(T // tm)),
        pl.BlockSpec((1, D), lambda i, j: (0, 0)),
        pl.BlockSpec((None, D, tn), lambda i, j: (layer, 0, j)),
    ]
    args = [x, gain.reshape(1, D), w]
    out_shape = [jax.ShapeDtypeStruct((T, n_out), out_dtype)]
    out_specs = [pl.BlockSpec((tm, tn), lambda i, j: (i, j))]
    if gate is not None:
        in_specs += [pl.BlockSpec((D, LANES), lambda i, j: (0, 0)),
                     pl.BlockSpec((1, LANES), lambda i, j: (0, 0))]
        args += list(gate)
        out_shape.append(jax.ShapeDtypeStruct((T, LANES), F32))
        out_specs.append(pl.BlockSpec((tm, LANES), lambda i, j: (i, 0)))
    res = pl.pallas_call(
        functools.partial(_rms_proj_kernel, tn=tn, q_cols=q_cols, q_scale=q_scale,
                          with_gate=gate is not None),
        grid=(T // tm, n_out // tn),
        in_specs=in_specs,
        out_specs=out_specs,
        out_shape=out_shape,
        scratch_shapes=[pltpu.VMEM((tm, D), BF16)],
        compiler_params=_params(),
        name="rms_proj_gate" if gate is not None else "rms_proj",
    )(*args)
    return res if gate is not None else res[0]


def _ffn_start(x_ref, g_ref, h_ref, o_ref):
    x = x_ref[...]
    h_ref[...] = _rmsnorm(x, g_ref[...]).astype(BF16)
    o_ref[...] = x


def _ffn_chunk(h_ref, base_ref, o_ref, wg, wu, wo):
    h = h_ref[...]
    gate = jnp.dot(h, wg, preferred_element_type=F32)
    up = jnp.dot(h, wu, preferred_element_type=F32)
    act = (gate * jax.nn.sigmoid(gate) * (0.5 * up)).astype(BF16)
    o_ref[...] = base_ref[...] + jnp.dot(act, wo, preferred_element_type=F32)


def _ffn_head_kernel(x_ref, g_ref, wg_ref, wu_ref, wo_ref, *rest, final_norm):
    if final_norm:
        fg_ref, o_ref, wg_out, wu_out, wo_out, h_ref = rest
    else:
        o_ref, wg_out, wu_out, wo_out, h_ref = rest
    j = pl.program_id(0)

    @pl.when(j == 0)
    def _():
        _ffn_start(x_ref, g_ref, h_ref, o_ref)

    wg, wu, wo = (w[...].astype(BF16) for w in (wg_ref, wu_ref, wo_ref))
    wg_out[...] = wg
    wu_out[...] = wu
    wo_out[...] = wo
    _ffn_chunk(h_ref, o_ref, o_ref, wg, wu, wo)

    if final_norm:
        @pl.when(j == pl.num_programs(0) - 1)
        def _():
            o_ref[...] = _rmsnorm(o_ref[...], fg_ref[...])


def _ffn_main_kernel(x_ref, g_ref, wg_ref, wu_ref, wo_ref, head_ref, *rest, final_norm, n_head):
    if final_norm:
        fg_ref, o_ref, h_ref = rest
    else:
        o_ref, h_ref = rest
    i = pl.program_id(0)
    j = pl.program_id(1)
    last = pl.num_programs(1) - 1

    @pl.when(jnp.logical_and(i < n_head, j == last))
    def _():
        o_ref[...] = head_ref[...]

    @pl.when(i >= n_head)
    def _():
        @pl.when(j == 0)
        def _():
            h_ref[...] = _rmsnorm(x_ref[...], g_ref[...]).astype(BF16)
            _ffn_chunk(h_ref, x_ref, o_ref, wg_ref[...], wu_ref[...], wo_ref[...])

        @pl.when(j > 0)
        def _():
            _ffn_chunk(h_ref, o_ref, o_ref, wg_ref[...], wu_ref[...], wo_ref[...])

        if final_norm:
            @pl.when(j == last)
            def _():
                o_ref[...] = _rmsnorm(o_ref[...], fg_ref[...])


def _ffn(x, gain, w_in, w_out, layer, *, tm, tf, head_tm, head_tf, final_gain=None):
    T, D = x.shape
    F = w_out.shape[1]
    assert T % tm == 0 and F % tf == 0 and F % head_tf == 0
    assert head_tm % tm == 0 and head_tm <= T
    final = final_gain is not None
    gain2 = gain.reshape(1, D)
    extra_specs = [pl.BlockSpec((1, D), lambda *_: (0, 0))] if final else []
    extra_args = [final_gain.reshape(1, D)] if final else []

    nfh = F // head_tf
    per = tf // head_tf
    assert tf % head_tf == 0
    once = dict(pipeline_mode=pl.Buffered(1))
    head, wg, wu, wo = pl.pallas_call(
        functools.partial(_ffn_head_kernel, final_norm=final),
        grid=(nfh,),
        in_specs=[
            pl.BlockSpec((head_tm, D), lambda j: (0, 0), **once),
            pl.BlockSpec((1, D), lambda j: (0, 0)),
            pl.BlockSpec((None, D, head_tf), lambda j: (layer, 0, j)),
            pl.BlockSpec((None, D, head_tf), lambda j: (layer, 0, j + nfh)),
            pl.BlockSpec((None, head_tf, D), lambda j: (layer, j, 0)),
        ] + extra_specs,
        out_specs=[
            pl.BlockSpec((head_tm, D), lambda j: (0, 0), **once),
            pl.BlockSpec((None, D, head_tf), lambda j: (j // per, 0, j % per)),
            pl.BlockSpec((None, D, head_tf), lambda j: (j // per, 0, j % per)),
            pl.BlockSpec((head_tf, D), lambda j: (j, 0)),
        ],
        out_shape=[
            jax.ShapeDtypeStruct((head_tm, D), F32),
            jax.ShapeDtypeStruct((F // tf, D, tf), BF16),
            jax.ShapeDtypeStruct((F // tf, D, tf), BF16),
            jax.ShapeDtypeStruct((F, D), BF16),
        ],
        scratch_shapes=[pltpu.VMEM((head_tm, D), BF16)],
        compiler_params=_params(),
        name="ffn_head_final" if final else "ffn_head",
    )(x, gain2, w_in, w_in, w_out, *extra_args)

    n_head = head_tm // tm

    def chunk(i, j):
        return jnp.where(i < n_head, 0, j)

    return pl.pallas_call(
        functools.partial(_ffn_main_kernel, final_norm=final, n_head=n_head),
        grid=(T // tm, F // tf),
        in_specs=[
            pl.BlockSpec((tm, D), lambda i, j: (i, 0)),
            pl.BlockSpec((1, D), lambda i, j: (0, 0)),
            pl.BlockSpec((None, D, tf), lambda i, j: (chunk(i, j), 0, 0)),
            pl.BlockSpec((None, D, tf), lambda i, j: (chunk(i, j), 0, 0)),
            pl.BlockSpec((tf, D), lambda i, j: (chunk(i, j), 0)),
            pl.BlockSpec((tm, D), lambda i, j: (jnp.minimum(i, n_head - 1), 0)),
        ] + extra_specs,
        out_specs=pl.BlockSpec((tm, D), lambda i, j: (i, 0)),
        out_shape=jax.ShapeDtypeStruct((T, D), F32),
        scratch_shapes=[pltpu.VMEM((tm, D), BF16)],
        compiler_params=_params(),
        name="ffn_final" if final else "ffn",
    )(x, gain2, wg, wu, wo, head, *extra_args)


def _cumsum_kernel(tri_ref, x_ref, ccol_ref, crow_ref, *, heads):
    tri = tri_ref[...]
    blk = tri.shape[0]
    carry = jnp.zeros((1, LANES), F32)
    for r0 in range(0, x_ref.shape[1], blk):
        pieces = _split3(x_ref[0, r0:r0 + blk, :] * LOG2E)
        c = carry + sum(jnp.dot(tri, p, preferred_element_type=F32) for p in pieces)
        ccol_ref[0, r0:r0 + blk, :] = c
        carry = c[blk - 1:blk, :]
    crow_ref[0] = ccol_ref[0].T[:heads]


def _fox_cumsum(logf, heads, blk=256):
    B, S, _ = logf.shape
    blk = min(blk, S)
    assert S % blk == 0
    tri = jnp.tril(jnp.ones((blk, blk), BF16))
    return pl.pallas_call(
        functools.partial(_cumsum_kernel, heads=heads),
        grid=(B,),
        in_specs=[pl.BlockSpec((blk, blk), lambda b: (0, 0)),
                  pl.BlockSpec((1, S, LANES), lambda b: (b, 0, 0))],
        out_specs=[pl.BlockSpec((1, S, LANES), lambda b: (b, 0, 0)),
                   pl.BlockSpec((1, heads, S), lambda b: (b, 0, 0))],
        out_shape=[jax.ShapeDtypeStruct((B, S, LANES), F32),
                   jax.ShapeDtypeStruct((B, heads, S), F32)],
        compiler_params=_params(),
        name="fox_cumsum",
    )(tri, logf)


def _fox_attn_kernel(q_ref, k_ref, v_ref, ccol_ref, crow_ref, o_ref, *, tile, hpb):
    hb = pl.program_id(1)
    lane = lax.broadcasted_iota(jnp.int32, (tile, LANES), 1)
    heads = range(hpb)
    for qi in range(q_ref.shape[1] // tile):
        rows = slice(qi * tile, (qi + 1) * tile)
        ccol = ccol_ref[0, rows, :]
        q = [q_ref[0, rows, e * HEAD_DIM:(e + 1) * HEAD_DIM] for e in heads]
        cq = [jnp.sum(jnp.where(lane == hb * hpb + e, ccol, 0.0), axis=1, keepdims=True)
              for e in heads]
        carry = [(jnp.full((tile, 1), NEG_INF, F32), jnp.zeros((tile, 1), F32),
                  jnp.zeros((tile, HEAD_DIM), F32)) for _ in heads]
        for j in range(qi + 1):
            keys = slice(j * tile, (j + 1) * tile)
            for e in heads:
                m, l, acc = carry[e]
                k = k_ref[0, keys, e * HEAD_DIM:(e + 1) * HEAD_DIM]
                v = v_ref[0, keys, e * HEAD_DIM:(e + 1) * HEAD_DIM]
                ck = crow_ref[0, e, j:j + 1, :]
                s = lax.dot_general(q[e], k, _NT, preferred_element_type=F32) - ck
                if j == qi:
                    row = lax.broadcasted_iota(jnp.int32, (tile, tile), 0)
                    col = lax.broadcasted_iota(jnp.int32, (tile, tile), 1)
                    s = jnp.where(col <= row, s, NEG_INF)
                m_new = jnp.maximum(m, jnp.max(s, axis=1, keepdims=True) + cq[e])
                alpha = jnp.exp2(m - m_new)
                p = jnp.exp2(s - (m_new - cq[e]))
                l = alpha * l + jnp.sum(p, axis=1, keepdims=True)
                acc = alpha * acc + jnp.dot(p.astype(BF16), v, preferred_element_type=F32)
                carry[e] = (m_new, l, acc)
        for e in heads:
            _, l, acc = carry[e]
            o_ref[0, rows, e * HEAD_DIM:(e + 1) * HEAD_DIM] = (acc / l).astype(o_ref.dtype)


def _fox_attention(qkv, ccol, crow, *, heads, tile, hpb=2):
    B, S, _ = qkv.shape
    hpb = min(hpb, heads)
    assert S % tile == 0 and heads % hpb == 0
    nq = S // tile
    nhb = heads // hpb
    w = hpb * HEAD_DIM
    crow4 = crow.reshape(B, heads, nq, tile)
    return pl.pallas_call(
        functools.partial(_fox_attn_kernel, tile=tile, hpb=hpb),
        grid=(B, nhb),
        in_specs=[
            pl.BlockSpec((1, S, w), lambda b, h: (b, 0, h)),
            pl.BlockSpec((1, S, w), lambda b, h: (b, 0, nhb + h)),
            pl.BlockSpec((1, S, w), lambda b, h: (b, 0, 2 * nhb + h)),
            pl.BlockSpec((1, S, LANES), lambda b, h: (b, 0, 0)),
            pl.BlockSpec((1, hpb, nq, tile), lambda b, h: (b, h, 0, 0)),
        ],
        out_specs=pl.BlockSpec((1, S, w), lambda b, h: (b, 0, h)),
        out_shape=jax.ShapeDtypeStruct((B, S, heads * HEAD_DIM), BF16),
        compiler_params=_params(),
        name="fox_attn",
    )(qkv, qkv, qkv, ccol, crow4)


def _t5_bucket(dist):
    max_exact = NUM_BUCKETS // 2
    d = np.maximum(dist, 1).astype(np.float32)
    large = max_exact + (np.log(d / max_exact) / np.log(MAX_DISTANCE / max_exact)
                         * (NUM_BUCKETS - max_exact)).astype(np.int32)
    large = np.minimum(large, NUM_BUCKETS - 1)
    return np.where(dist < max_exact, dist, large).astype(np.int32)


def _dil_bias_tables(rel_bias, patterns, group_heads):
    period = 4 * DIL_TILE
    c = np.arange(period)
    back = c < 2 * DIL_TILE
    groups = []
    for g, (window, dil) in enumerate(patterns):
        assert window // dil == DIL_TILE
        strips = []
        for t in (0, 1):
            delta = np.where(back, t * DIL_TILE - c, t * DIL_TILE + period - c)
            valid = (delta >= 0) & (delta <= DIL_TILE) & (back | (c > period - DIL_TILE))
            bucket = _t5_bucket(np.clip(delta, 0, DIL_TILE) * dil)
            vec = rel_bias[bucket][:, g * group_heads:(g + 1) * group_heads].astype(F32)
            vec = jnp.where(valid[:, None], vec, NEG_INF).T
            flat = jnp.tile(vec, (1, DIL_TILE))[:, :DIL_TILE * (period - 1)]
            strips.append(flat.reshape(group_heads, DIL_TILE, period - 1)[:, :, :2 * DIL_TILE])
        groups.append(jnp.stack(strips, axis=1))
    return jnp.stack(groups)


def _dil_attn_kernel(*refs, dils, seq):
    n_g = len(dils)
    q_refs, k_refs, v_refs = refs[:n_g], refs[n_g:2 * n_g], refs[2 * n_g:3 * n_g]
    t_ref = refs[3 * n_g]
    o_refs = refs[3 * n_g + 1:4 * n_g + 1]
    o_sc, lse_sc = refs[4 * n_g + 1:]

    def rows(start, size, stride):
        return pl.ds(start, size) if stride == 1 else pl.ds(start, size, stride=stride)

    ones = jnp.ones((2 * DIL_TILE, LANES), BF16)

    for g, d in enumerate(dils):
        cls_len = seq // d
        n_keys = min(2 * DIL_TILE, cls_len)
        for r in range(d):
            for u0 in range(0, cls_len, DIL_TILE):
                s0 = min(max(u0 - DIL_TILE, 0), cls_len - n_keys)
                strip = (u0 - s0) // DIL_TILE
                q_rows = rows(r + d * u0, DIL_TILE, d)
                k_rows = rows(r + d * s0, n_keys, d)
                q = q_refs[g][0, q_rows, :].astype(BF16)
                k = k_refs[g][0, k_rows, :].astype(BF16)
                v = v_refs[g][0, k_rows, :].astype(BF16)
                s = lax.dot_general(q, k, _NT, preferred_element_type=F32)
                s = s + t_ref[g, 0, strip][:, :n_keys]
                m = jnp.max(s, axis=1, keepdims=True)
                p = jnp.exp(s - m).astype(BF16)
                acc = jnp.dot(p, jnp.concatenate([v, ones[:n_keys]], axis=1),
                              preferred_element_type=F32)
                l = acc[:, HEAD_DIM:]
                o_sc[g, q_rows, :] = acc[:, :HEAD_DIM] / l
                lse_sc[g, q_rows, :] = jnp.broadcast_to(m + jnp.log(l), (DIL_TILE, LANES))

    lse = [lse_sc[g] for g in range(n_g)]
    mx = functools.reduce(jnp.maximum, lse)
    e = [jnp.exp(x - mx) for x in lse]
    inv = 1.0 / functools.reduce(lambda x, y: x + y, e)
    for g in range(n_g):
        o_refs[g][0] = (o_sc[g] * (e[g] * inv)).astype(o_refs[g].dtype)


def _dil_attention(qkv, tables, *, patterns, group_heads):
    B, S, _ = qkv.shape
    n_g = len(patterns)
    dils = tuple(d for _, d in patterns)
    assert all(S % (d * DIL_TILE) == 0 for d in dils)
    nh = n_g * group_heads

    def head_spec(part, g):
        return pl.BlockSpec((1, S, HEAD_DIM),
                            lambda b, h, part=part, g=g: (b, 0, part * nh + g * group_heads + h))

    in_specs = [head_spec(part, g) for part in range(3) for g in range(n_g)]
    in_specs.append(pl.BlockSpec((n_g, 1, 2, DIL_TILE, 2 * DIL_TILE),
                                 lambda b, h: (0, h, 0, 0, 0)))
    out_spec = pl.BlockSpec((1, S, HEAD_DIM), lambda b, h: (b, 0, h))
    return pl.pallas_call(
        functools.partial(_dil_attn_kernel, dils=dils, seq=S),
        grid=(B, group_heads),
        in_specs=in_specs,
        out_specs=[out_spec] * n_g,
        out_shape=[jax.ShapeDtypeStruct((B, S, group_heads * HEAD_DIM), BF16)] * n_g,
        scratch_shapes=[pltpu.VMEM((n_g, S, HEAD_DIM), F32), pltpu.VMEM((n_g, S, LANES), F32)],
        compiler_params=_params(),
        name="dil_attn",
    )(*([qkv] * (3 * n_g)), tables)


def _mix_cross_kernel(*refs, n_a, heads, scale):
    a_refs, wm_refs = refs[:n_a], refs[n_a:2 * n_a]
    x_ref, g_ref, wq_ref, kv_ref, wo_ref, o_ref = refs[2 * n_a:]
    x = x_ref[0]
    for a_ref, wm_ref in zip(a_refs, wm_refs):
        x = x + jnp.dot(a_ref[0], wm_ref[...], preferred_element_type=F32)
    h = _rmsnorm(x, g_ref[...]).astype(BF16)
    q = (jnp.dot(h, wq_ref[...].astype(BF16), preferred_element_type=F32) * scale).astype(BF16)
    kv = kv_ref[0]
    outs = []
    for hd in range(heads):
        qh = q[:, hd * HEAD_DIM:(hd + 1) * HEAD_DIM]
        kh = kv[:, hd * HEAD_DIM:(hd + 1) * HEAD_DIM]
        vh = kv[:, (heads + hd) * HEAD_DIM:(heads + hd + 1) * HEAD_DIM]
        s = lax.dot_general(qh, kh, _NT, preferred_element_type=F32)
        p = jnp.exp(s - jnp.max(s, axis=1, keepdims=True))
        l = jnp.sum(p, axis=1, keepdims=True)
        outs.append((jnp.dot(p.astype(BF16), vh, preferred_element_type=F32) / l).astype(BF16))
    o = jnp.concatenate(outs, axis=1)
    o_ref[0] = x + jnp.dot(o, wo_ref[...].astype(BF16), preferred_element_type=F32)


def _mix_cross(x, a_list, w_mix, mix_layer, gain, wq, kv, wo, layer, *, heads, tm):
    B, S, D = x.shape
    M = kv.shape[1]
    hd = heads * HEAD_DIM
    kg = a_list[0].shape[2]
    n_a = len(a_list)
    assert S % tm == 0 and kg * n_a == w_mix.shape[1]
    once = dict(pipeline_mode=pl.Buffered(1))
    in_specs = [pl.BlockSpec((1, tm, kg), lambda b, i: (b, i, 0)) for _ in a_list]
    in_specs += [pl.BlockSpec((None, kg, D), lambda b, i, g=g: (mix_layer, g, 0), **once)
                 for g in range(n_a)]
    in_specs += [
        pl.BlockSpec((1, tm, D), lambda b, i: (b, i, 0)),
        pl.BlockSpec((1, D), lambda b, i: (0, 0)),
        pl.BlockSpec((None, D, hd), lambda b, i: (layer, 0, 0), **once),
        pl.BlockSpec((1, M, 2 * hd), lambda b, i: (b, 0, 0)),
        pl.BlockSpec((None, hd, D), lambda b, i: (layer, 0, 0), **once),
    ]
    return pl.pallas_call(
        functools.partial(_mix_cross_kernel, n_a=n_a, heads=heads, scale=HEAD_DIM ** -0.5),
        grid=(B, S // tm),
        in_specs=in_specs,
        out_specs=pl.BlockSpec((1, tm, D), lambda b, i: (b, i, 0)),
        out_shape=jax.ShapeDtypeStruct((B, S, D), F32),
        compiler_params=_params(),
        name="mix_cross",
    )(*a_list, *([w_mix] * n_a), x, gain.reshape(1, D), wq, kv, wo)


def _fox_mixer(x, gain, w_in, w_gate, b_f, layer, *, heads, attn_tile, tm):
    B, S, D = x.shape
    hd = heads * HEAD_DIM
    w_f = jnp.pad(w_gate, ((0, 0), (0, LANES - heads))).astype(BF16)
    b_pad = jnp.pad(b_f, (0, LANES - heads)).reshape(1, LANES).astype(F32)
    qkv, logf = _rms_proj(x.reshape(B * S, D), gain, w_in, layer, n_out=3 * hd, tm=tm,
                          tn=min(1024, hd), out_dtype=BF16, q_cols=hd,
                          q_scale=HEAD_DIM ** -0.5 * LOG2E, gate=(w_f, b_pad))
    ccol, crow = _fox_cumsum(logf.reshape(B, S, LANES), heads)
    return [_fox_attention(qkv.reshape(B, S, 3 * hd), ccol, crow, heads=heads, tile=attn_tile)]


def _dilated_mixer(x, gain, w_in, layer, rel_bias, *, patterns, group_heads, tm):
    B, S, D = x.shape
    n_g = len(patterns)
    hd = n_g * group_heads * HEAD_DIM
    qkv = _rms_proj(x.reshape(B * S, D), gain, w_in, layer, n_out=3 * hd, tm=tm,
                    tn=group_heads * HEAD_DIM, out_dtype=F32, q_cols=hd, q_scale=HEAD_DIM ** -0.5)
    tables = _dil_bias_tables(rel_bias, patterns, group_heads)
    return _dil_attention(qkv.reshape(B, S, 3 * hd), tables, patterns=patterns,
                          group_heads=group_heads)


def _forward(x, mem, ffn1_norm, ffn1_w_in, ffn1_w_out, mix_norm, fox_w_in, fox_b_f, fox_w_out,
             dil_w_in, dil_w_out, rel_bias, cross_norm, mem_norm, cross_w_q, cross_w_kv,
             cross_w_out, ffn2_norm, ffn2_w_in, ffn2_w_out, final_norm, *,
             patterns, fox_heads, group_heads, cross_heads, tm, tf, head_tm, head_tf, proj_tm,
             attn_tile, cross_tm):
    B, S, D = x.shape
    M = mem.shape[1]
    depth = ffn1_norm.shape[0]
    n_mixers = 2
    fox_hd = fox_heads * HEAD_DIM
    cross_hd = cross_heads * HEAD_DIM
    fox_w_gate = fox_w_in[:, :, 3 * fox_hd:]
    fox_w_in, fox_w_out, dil_w_in, dil_w_out, cross_w_kv = (
        w.astype(BF16) for w in (fox_w_in, fox_w_out, dil_w_in, dil_w_out, cross_w_kv))
    ffn_tiles = dict(tm=tm, tf=tf, head_tm=head_tm, head_tf=head_tf)
    for i in range(depth):
        x = _ffn(x.reshape(B * S, D), ffn1_norm[i], ffn1_w_in, ffn1_w_out, i,
                 **ffn_tiles).reshape(B, S, D)
        j = i // n_mixers
        if i % n_mixers == 0:
            mixed = _fox_mixer(x, mix_norm[i], fox_w_in, fox_w_gate[j], fox_b_f[j], j,
                               heads=fox_heads, attn_tile=attn_tile, tm=proj_tm)
            w_mix = fox_w_out
        else:
            mixed = _dilated_mixer(x, mix_norm[i], dil_w_in, j, rel_bias,
                                   patterns=patterns, group_heads=group_heads, tm=proj_tm)
            w_mix = dil_w_out
        kv = _rms_proj(mem.reshape(B * M, D), mem_norm, cross_w_kv, i, n_out=2 * cross_hd,
                       tm=min(tm, B * M), tn=min(512, cross_hd), out_dtype=BF16)
        x = _mix_cross(x, mixed, w_mix, j, cross_norm[i], cross_w_q,
                       kv.reshape(B, M, 2 * cross_hd), cross_w_out, i,
                       heads=cross_heads, tm=cross_tm)
        x = _ffn(x.reshape(B * S, D), ffn2_norm[i], ffn2_w_in, ffn2_w_out, i, **ffn_tiles,
                 final_gain=final_norm if i == depth - 1 else None).reshape(B, S, D)
    return x


def kernel(x, mem, ffn1_norm, ffn1_w_in, ffn1_w_out, mix_norm, fox_w_in, fox_b_f, fox_w_out, dil_w_in, dil_w_out, rel_bias, cross_norm, mem_norm, cross_w_q, cross_w_kv, cross_w_out, ffn2_norm, ffn2_w_in, ffn2_w_out, final_norm):
    return _forward(x, mem, ffn1_norm, ffn1_w_in, ffn1_w_out, mix_norm, fox_w_in, fox_b_f,
                    fox_w_out, dil_w_in, dil_w_out, rel_bias, cross_norm, mem_norm, cross_w_q,
                    cross_w_kv, cross_w_out, ffn2_norm, ffn2_w_in, ffn2_w_out, final_norm,
                    patterns=DIL_PATTERNS, fox_heads=FOX_HEADS, group_heads=DIL_GROUP_HEADS,
                    cross_heads=CROSS_HEADS, tm=512, tf=512, head_tm=1024, head_tf=256,
                    proj_tm=1024, attn_tile=512, cross_tm=512)
```
